```python
import math
import jax
import jax.numpy as jnp
from jax import lax

D_MODEL = 1024
BATCH = 2
SEQ = 8192
DEPTH = 4

GRID_W = 64
CTX_LEN = 256
HEAD_DIM = 64
A_HEADS = 4
B_HEADS = 8
B_KV = 2
C_HEADS = 8
C_KV = 2
WINDOW = 128
QBLK = 128
D_FF = 4 * D_MODEL
ROPE_THETA = 10000.0
EPS = 1e-6
SUBLN_EPS = 1e-5

A_W = A_HEADS * 2 * HEAD_DIM
B_W = B_HEADS * HEAD_DIM
C_W = C_HEADS * HEAD_DIM
MIX_W = A_W + B_W + C_W
B_KW = B_KV * HEAD_DIM
C_KW = C_KV * HEAD_DIM
IN_SIZES = (A_W, A_W, A_W, B_W, B_KW, B_KW, C_W, C_KW, C_KW, 3 * D_MODEL)
IN_SPLITS = tuple(sum(IN_SIZES[:i + 1]) for i in range(len(IN_SIZES) - 1))
IN_W = sum(IN_SIZES)

kernel_name = 'hybrid_diffusion_trunk'


def rms_norm(x, gain, eps=EPS):
    x32 = x.astype(jnp.float32)
    y = x32 * lax.rsqrt(jnp.mean(jnp.square(x32), axis=-1, keepdims=True) + eps)
    return (y * gain.astype(jnp.float32)).astype(x.dtype)


def modulate(x, gain, shift, scale):
    return rms_norm(x, gain) * (1 + scale) + shift


def axial_tables(rows):
    row = jnp.repeat(jnp.arange(rows, dtype=jnp.int32), GRID_W).astype(jnp.float32)
    col = jnp.tile(jnp.arange(GRID_W, dtype=jnp.int32), rows).astype(jnp.float32)
    half = HEAD_DIM // 2
    inv_freq = jnp.power(ROPE_THETA, -jnp.arange(0, half, 2, dtype=jnp.float32) / half)
    ang_r = row[:, None] * inv_freq[None, :]
    ang_c = col[:, None] * inv_freq[None, :]
    return (jnp.cos(ang_r)[None, :, None, :], jnp.sin(ang_r)[None, :, None, :],
            jnp.cos(ang_c)[None, :, None, :], jnp.sin(ang_c)[None, :, None, :])


def _rot(x, cos, sin):
    x1, x2 = jnp.split(x, 2, axis=-1)
    return jnp.concatenate([x1 * cos - x2 * sin, x2 * cos + x1 * sin], axis=-1)


def rope_2d(x, tabs):
    cos_r, sin_r, cos_c, sin_c = tabs
    xr, xc = jnp.split(x.astype(jnp.float32), 2, axis=-1)
    return jnp.concatenate([_rot(xr, cos_r, sin_r), _rot(xc, cos_c, sin_c)], axis=-1).astype(x.dtype)


def rope_pairs(x, tabs):
    b, n, h, m, d = x.shape
    return rope_2d(x.reshape(b, n, h * m, d), tabs).reshape(b, n, h, m, d)


def split_proj(p):
    b, n, _ = p.shape
    qa, ka, va, qb, kb, vb, qc, kc, vc, gt = jnp.split(p, IN_SPLITS, axis=-1)
    return (qa.reshape(b, n, A_HEADS, 2, HEAD_DIM), ka.reshape(b, n, A_HEADS, 2, HEAD_DIM),
            va.reshape(b, n, A_HEADS, 2 * HEAD_DIM),
            qb.reshape(b, n, B_HEADS, HEAD_DIM), kb.reshape(b, n, B_KV, HEAD_DIM), vb.reshape(b, n, B_KV, HEAD_DIM),
            qc.reshape(b, n, C_HEADS, HEAD_DIM), kc.reshape(b, n, C_KV, HEAD_DIM), vc.reshape(b, n, C_KV, HEAD_DIM),
            gt)


def gqa_attend(q, k, v, sink=None):
    b, sq, h, d = q.shape
    kvh = k.shape[2]
    g = h // kvh
    nblk = sq // QBLK
    scale = d ** -0.5
    qb = q.reshape(b, nblk, QBLK, kvh, g, d).transpose(1, 0, 2, 3, 4, 5)

    def block(qi):
        s = jnp.einsum('bqngd,bknd->bngqk', qi, k, preferred_element_type=jnp.float32) * scale
        if sink is None:
            p = jax.nn.softmax(s, axis=-1)
        else:
            sk = sink.astype(jnp.float32).reshape(1, kvh, g, 1, 1)
            m = jnp.maximum(jnp.max(s, axis=-1, keepdims=True), sk)
            e = jnp.exp(s - m)
            p = e / (jnp.sum(e, axis=-1, keepdims=True) + jnp.exp(sk - m))
        return jnp.einsum('bngqk,bkne->bqnge', p.astype(v.dtype), v)

    o = lax.map(block, qb)
    return o.transpose(1, 0, 2, 3, 4, 5).reshape(b, sq, h, v.shape[-1])


def diff_attend(q, k, v, lam):
    b, sq, h, _, d = q.shape
    nblk = sq // QBLK
    scale = d ** -0.5
    qb = q.reshape(b, nblk, QBLK, h, 2, d).transpose(1, 0, 2, 3, 4, 5)

    def block(qi):
        s = jnp.einsum('bqhmd,bkhmd->bhmqk', qi, k, preferred_element_type=jnp.float32) * scale
        p = jax.nn.softmax(s, axis=-1)
        pd = p[:, :, 0] - lam * p[:, :, 1]
        return jnp.einsum('bhqk,bkhe->bqhe', pd.astype(v.dtype), v)

    o = lax.map(block, qb)
    return o.transpose(1, 0, 2, 3, 4).reshape(b, sq, h, v.shape[-1])


def window_attend(q, k, v, kc, vc, sink):
    b, s, h, d = q.shape
    kvh = k.shape[2]
    g = h // kvh
    nblk = s // QBLK
    scale = d ** -0.5

    def band(t):
        tp = jnp.pad(t, ((0, 0), (QBLK, QBLK), (0, 0), (0, 0))).reshape(b, nblk + 2, QBLK, kvh, t.shape[-1])
        return jnp.concatenate([tp[:, :-2], tp[:, 1:-1], tp[:, 2:]], axis=2)

    kband, vband = band(k), band(v)
    qb = q.reshape(b, nblk, QBLK, kvh, g, d)
    s_loc = jnp.einsum('bjqngd,bjknd->bjngqk', qb, kband, preferred_element_type=jnp.float32) * scale
    s_ctx = jnp.einsum('bjqngd,bknd->bjngqk', qb, kc, preferred_element_type=jnp.float32) * scale
    blk = jnp.arange(nblk, dtype=jnp.int32)[:, None, None]
    qpos = blk * QBLK + jnp.arange(QBLK, dtype=jnp.int32)[None, :, None]
    kpos = (blk - 1) * QBLK + jnp.arange(3 * QBLK, dtype=jnp.int32)[None, None, :]
    valid = (jnp.abs(kpos - qpos) <= WINDOW) & (kpos >= 0) & (kpos < s)
    s_loc = jnp.where(valid[None, :, None, None], s_loc, -jnp.inf)
    sk = sink.astype(jnp.float32).reshape(1, 1, kvh, g, 1, 1)
    m = jnp.maximum(jnp.maximum(jnp.max(s_loc, axis=-1, keepdims=True), jnp.max(s_ctx, axis=-1, keepdims=True)), sk)
    e_loc = jnp.exp(s_loc - m)
    e_ctx = jnp.exp(s_ctx - m)
    denom = jnp.sum(e_loc, axis=-1, keepdims=True) + jnp.sum(e_ctx, axis=-1, keepdims=True) + jnp.exp(sk - m)
    o = (jnp.einsum('bjngqk,bjkne->bjqnge', (e_loc / denom).astype(v.dtype), vband)
         + jnp.einsum('bjngqk,bkne->bjqnge', (e_ctx / denom).astype(v.dtype), vc))
    return o.reshape(b, s, h, v.shape[-1])


def diff_post(o, g_subln, lam_init):
    b, n = o.shape[:2]
    return (rms_norm(o, g_subln, SUBLN_EPS) * (1.0 - lam_init)).reshape(b, n, A_W)


def flat_heads(o):
    b, n = o.shape[:2]
    return o.reshape(b, n, -1)


def merge_branches(oa, ob, oc, gt, w_branch, w_out):
    ga, gb, gc = jnp.split(jax.nn.sigmoid(gt), 3, axis=-1)
    z = (ga * (oa @ w_branch[:A_W]) + gb * (ob @ w_branch[A_W:A_W + B_W])
         + gc * (oc @ w_branch[A_W + B_W:]))
    return z @ w_out


def sq_relu_mlp(h, w1, w2):
    return jnp.square(jax.nn.relu(h @ w1)) @ w2


def setup_inputs(seed: int = 0) -> dict:
    key = jax.random.key(seed)
    ks = jax.random.split(key, 24)
    f32 = jnp.float32
    L = DEPTH

    def nrm(k, shape, scale):
        return jax.random.normal(k, shape, f32) * scale

    return {
        'x': nrm(ks[0], (BATCH, SEQ, D_MODEL), 1.0),
        'c': nrm(ks[1], (BATCH, D_MODEL), 1.0),
        'ctx': nrm(ks[2], (BATCH, CTX_LEN, D_MODEL), 1.0),
        'c_ctx': nrm(ks[3], (D_MODEL,), 1.0),
        'w_ada': nrm(ks[4], (L, D_MODEL, 6 * D_MODEL), 0.5 * D_MODEL ** -0.5),
        'b_ada': nrm(ks[5], (L, 6 * D_MODEL), 0.01),
        'g_pre_mix': 1.0 + nrm(ks[6], (L, D_MODEL), 0.02),
        'g_post_mix': 1.0 + nrm(ks[7], (L, D_MODEL), 0.02),
        'g_pre_ff': 1.0 + nrm(ks[8], (L, D_MODEL), 0.02),
        'g_post_ff': 1.0 + nrm(ks[9], (L, D_MODEL), 0.02),
        'w_in': nrm(ks[10], (L, D_MODEL, IN_W), D_MODEL ** -0.5),
        'g_qnorm': 1.0 + nrm(ks[11], (L, HEAD_DIM), 0.02),
        'g_knorm': 1.0 + nrm(ks[12], (L, HEAD_DIM), 0.02),
        'lam_q1': nrm(ks[13], (L, HEAD_DIM), 0.1),
        'lam_k1': nrm(ks[14], (L, HEAD_DIM), 0.1),
        'lam_q2': nrm(ks[15], (L, HEAD_DIM), 0.1),
        'lam_k2': nrm(ks[16], (L, HEAD_DIM), 0.1),
        'g_subln': 1.0 + nrm(ks[17], (L, 2 * HEAD_DIM), 0.02),
        'sink': nrm(ks[18], (L, C_HEADS), 0.5),
        'w_branch': nrm(ks[19], (L, MIX_W, D_MODEL), A_W ** -0.5),
        'w_out': nrm(ks[20], (L, D_MODEL, D_MODEL), D_MODEL ** -0.5),
        'w_ff1': nrm(ks[21], (L, D_MODEL, D_FF), D_MODEL ** -0.5),
        'w_ff2': nrm(ks[22], (L, D_FF, D_MODEL), D_FF ** -0.5),
    }


def reference(x, c, ctx, c_ctx, w_ada, b_ada, g_pre_mix, g_post_mix, g_pre_ff, g_post_ff, w_in, g_qnorm, g_knorm,
              lam_q1, lam_k1, lam_q2, lam_k2, g_subln, sink, w_branch, w_out, w_ff1, w_ff2):
    n_tok = x.shape[1]
    rows = n_tok // GRID_W
    tabs = axial_tables(rows)
    y = ctx
    silu_c = jax.nn.silu(c)
    silu_cc = jax.nn.silu(c_ctx)
    for l in range(DEPTH):
        lam_init = 0.8 - 0.6 * math.exp(-0.3 * l)
        lam = (jnp.exp(jnp.sum(lam_q1[l].astype(jnp.float32) * lam_k1[l].astype(jnp.float32)))
               - jnp.exp(jnp.sum(lam_q2[l].astype(jnp.float32) * lam_k2[l].astype(jnp.float32))) + lam_init)
        sh1, sc1, gm1, sh2, sc2, gm2 = jnp.split((silu_c @ w_ada[l] + b_ada[l])[:, None, :], 6, axis=-1)
        csh1, csc1, cgm1, csh2, csc2, cgm2 = jnp.split((silu_cc @ w_ada[l] + b_ada[l])[None, None, :], 6, axis=-1)

        qa_x, ka_x, va_x, qb_x, kb_x, vb_x, qc_x, kc_x, vc_x, gt_x = split_proj(
            modulate(x, g_pre_mix[l], sh1, sc1) @ w_in[l])
        qa_y, ka_y, va_y, qb_y, kb_y, vb_y, qc_y, kc_y, vc_y, gt_y = split_proj(
            modulate(y, g_pre_mix[l], csh1, csc1) @ w_in[l])
        qa_x = rope_pairs(qa_x, tabs)
        ka_x = rope_pairs(ka_x, tabs)
        qb_x = rope_2d(rms_norm(qb_x, g_qnorm[l]), tabs)
        kb_x = rope_2d(rms_norm(kb_x, g_knorm[l]), tabs)
        kb_y = rms_norm(kb_y, g_knorm[l])
        qc_x = rope_2d(qc_x, tabs)
        kc_x = rope_2d(kc_x, tabs)

        oa_x = diff_attend(qa_x, jnp.concatenate([ka_x, ka_y], axis=1), jnp.concatenate([va_x, va_y], axis=1), lam)
        ob_x = gqa_attend(qb_x, jnp.concatenate([kb_x, kb_y], axis=1), jnp.concatenate([vb_x, vb_y], axis=1))
        oc_x = window_attend(qc_x, kc_x, vc_x, kc_y, vc_y, sink[l])
        mix_x = merge_branches(diff_post(oa_x, g_subln[l], lam_init), flat_heads(ob_x), flat_heads(oc_x),
                               gt_x, w_branch[l], w_out[l])
        x = x + gm1 * rms_norm(mix_x, g_post_mix[l])
        x = x + gm2 * rms_norm(sq_relu_mlp(modulate(x, g_pre_ff[l], sh2, sc2), w_ff1[l], w_ff2[l]), g_post_ff[l])

        if l < DEPTH - 1:
            qb_y = rms_norm(qb_y, g_qnorm[l])
            oa_y = diff_attend(qa_y, ka_y, va_y, lam)
            ob_y = gqa_attend(qb_y, kb_y, vb_y)
            oc_y = gqa_attend(qc_y, kc_y, vc_y, sink[l])
            mix_y = merge_branches(diff_post(oa_y, g_subln[l], lam_init), flat_heads(ob_y), flat_heads(oc_y),
                                   gt_y, w_branch[l], w_out[l])
            y = y + cgm1 * rms_norm(mix_y, g_post_mix[l])
            y = y + cgm2 * rms_norm(sq_relu_mlp(modulate(y, g_pre_ff[l], csh2, csc2), w_ff1[l], w_ff2[l]),
                                    g_post_ff[l])
    return x
```

```python
import functools
import math

import jax
import jax.numpy as jnp
from jax import lax
from jax.experimental import pallas as pl
from jax.experimental.pallas import tpu as pltpu

HEAD_DIM = 64
A_HEADS = 4
B_HEADS = 8
B_KV = 2
C_HEADS = 8
C_KV = 2
WINDOW = 128
GRID_W = 64
ROPE_THETA = 10000.0
EPS = 1e-6
SUBLN_EPS = 1e-5

A_W = A_HEADS * 2 * HEAD_DIM
B_W = B_HEADS * HEAD_DIM
C_W = C_HEADS * HEAD_DIM
B_KW = B_KV * HEAD_DIM
C_KW = C_KV * HEAD_DIM
KV_PAIR = 2 * HEAD_DIM

CH = 256
MOD_ROWS = 8
NEG = -1e30
VMEM_LIMIT = 48 * 1024 * 1024

F32 = jnp.float32
BF16 = jnp.bfloat16


def _cparams(*sem):
    return pltpu.CompilerParams(dimension_semantics=sem, vmem_limit_bytes=VMEM_LIMIT)


def _const_spec(shape):
    nd = len(shape)
    return pl.BlockSpec(shape, lambda *_: (0,) * nd, pipeline_mode=pl.Buffered(1))


def _rms(x, gain, eps):
    return x * lax.rsqrt(jnp.mean(x * x, axis=-1, keepdims=True) + eps) * gain


def _modulate(x, gain, shift, scale):
    return _rms(x, gain, EPS) * (1.0 + scale) + shift


def _mod_kernel(c_ref, w_ref, b_ref, o_ref):
    c = c_ref[...]
    s = c * jax.nn.sigmoid(c)
    o_ref[0] = jnp.dot(s, w_ref[0], preferred_element_type=F32,
                       precision=lax.Precision.HIGHEST) + b_ref[0]


def _mod_call(cvec, w_ada, b_ada):
    depth, d, n = w_ada.shape
    tn = n // 4
    return pl.pallas_call(
        _mod_kernel,
        grid=(depth, n // tn),
        in_specs=[pl.BlockSpec((MOD_ROWS, d), lambda l, j: (0, 0)),
                  pl.BlockSpec((1, d, tn), lambda l, j: (l, 0, j)),
                  pl.BlockSpec((1, 1, tn), lambda l, j: (l, 0, j))],
        out_specs=pl.BlockSpec((1, MOD_ROWS, tn), lambda l, j: (l, 0, j)),
        out_shape=jax.ShapeDtypeStruct((depth, MOD_ROWS, n), F32),
        compiler_params=_cparams("parallel", "parallel"),
        name="adaln_vectors",
    )(cvec, w_ada, b_ada.reshape(depth, 1, n))


def _rope_t(y3, cos_t, sin_t):
    part = jnp.concatenate([y3[:, 16:32], y3[:, 0:16], y3[:, 48:64], y3[:, 32:48]], axis=1)
    return y3 * cos_t[None] + part * sin_t[None]


def _rope_n(y, cos_n, sin_lo, sin_hi):
    return (y * cos_n + pltpu.roll(y, KV_PAIR - 16, axis=1) * sin_lo
            + pltpu.roll(y, 16, axis=1) * sin_hi)


def _inproj_kernel(z_ref, mod_ref, g_ref, wt_ref, wn_ref, rt_ref, rn_ref, gq_ref, gk_ref,
                   qa_ref, va_ref, qb_ref, vb_ref, qc_ref, vc_ref, ka_ref, kb_ref, kc_ref):
    m = mod_ref[...]
    xb = _modulate(z_ref[0], g_ref[...], m[0:1], m[1:2]).astype(BF16)
    cos_t, sin_t = rt_ref[0, 0], rt_ref[0, 1]
    cos_n, sin_lo, sin_hi = rn_ref[0], rn_ref[1], rn_ref[2]
    ch = xb.shape[0]
    q_scale = HEAD_DIM ** -0.5

    def proj_t(lo, hi):
        return lax.dot_general(wt_ref[lo:hi, :], xb, (((1,), (1,)), ((), ())),
                               preferred_element_type=F32)

    def store_q(ref, y3):
        ref[0, 0] = (_rope_t(y3, cos_t, sin_t) * q_scale).reshape(-1, ch).astype(BF16)

    o = 0
    store_q(qa_ref, proj_t(o, o + A_W).reshape(-1, HEAD_DIM, ch))
    o += A_W
    va_ref[0, 0] = proj_t(o, o + A_W).astype(BF16)
    o += A_W
    qb = proj_t(o, o + B_W).reshape(-1, HEAD_DIM, ch)
    qb = qb * lax.rsqrt(jnp.mean(qb * qb, axis=1, keepdims=True) + EPS) * gq_ref[...][None]
    store_q(qb_ref, qb)
    o += B_W
    vb_ref[0, 0] = proj_t(o, o + B_KW).astype(BF16)
    o += B_KW
    store_q(qc_ref, proj_t(o, o + C_W).reshape(-1, HEAD_DIM, ch))
    o += C_W
    vc_ref[0, 0] = proj_t(o, o + C_KW).astype(BF16)

    kn = jnp.dot(xb, wn_ref[...], preferred_element_type=F32)
    for i in range(A_W // KV_PAIR):
        ka_ref[0, :, i * KV_PAIR:(i + 1) * KV_PAIR] = _rope_n(
            kn[:, i * KV_PAIR:(i + 1) * KV_PAIR], cos_n, sin_lo, sin_hi).astype(BF16)
    kb = kn[:, A_W:A_W + B_KW]
    sq = kb * kb
    first = lax.broadcasted_iota(jnp.int32, sq.shape, 1) < HEAD_DIM
    s0 = jnp.sum(jnp.where(first, sq, 0.0), axis=-1, keepdims=True)
    s1 = jnp.sum(jnp.where(first, 0.0, sq), axis=-1, keepdims=True)
    ms = jnp.where(first, s0, s1) * (1.0 / HEAD_DIM)
    kb = kb * lax.rsqrt(ms + EPS) * gk_ref[...]
    kb_ref[0] = _rope_n(kb, cos_n, sin_lo, sin_hi).astype(BF16)
    kc_ref[0] = _rope_n(kn[:, A_W + B_KW:], cos_n, sin_lo, sin_hi).astype(BF16)


def _mod_index(nl, nb):
    return lambda b, j: (jnp.where(j < nl, b, nb), 0, 0)


def _inproj_call(z, mod, g_pre, wt, wn, rope_t, rope_n, gq_t, gk_n, nl):
    nb, t, d = z.shape
    nt = t // CH
    tok = lambda w: pl.BlockSpec((1, 1, w, CH), lambda b, j: (b, j, 0, 0))
    nat = lambda w: pl.BlockSpec((1, CH, w), lambda b, j: (b, j, 0))
    tshape = lambda w: jax.ShapeDtypeStruct((nb, nt, w, CH), BF16)
    nshape = lambda w: jax.ShapeDtypeStruct((nb, t, w), BF16)
    return pl.pallas_call(
        _inproj_kernel,
        grid=(nb, nt),
        in_specs=[pl.BlockSpec((1, CH, d), lambda b, j: (b, j, 0)),
                  pl.BlockSpec((None, 6, d), _mod_index(nl, nb)),
                  _const_spec((1, d)),
                  _const_spec(wt.shape),
                  _const_spec(wn.shape),
                  pl.BlockSpec((1, 2, HEAD_DIM, CH), lambda b, j: (j, 0, 0, 0)),
                  pl.BlockSpec((3, CH, KV_PAIR), lambda b, j: (0, j, 0)),
                  _const_spec(gq_t.shape),
                  _const_spec(gk_n.shape)],
        out_specs=[tok(A_W), tok(A_W), tok(B_W), tok(B_KW), tok(C_W), tok(C_KW),
                   nat(A_W), nat(B_KW), nat(C_KW)],
        out_shape=[tshape(A_W), tshape(A_W), tshape(B_W), tshape(B_KW), tshape(C_W), tshape(C_KW),
                   nshape(A_W), nshape(B_KW), nshape(C_KW)],
        compiler_params=_cparams("parallel", "parallel"),
        name="in_projection",
    )(z, mod, g_pre, wt, wn, rope_t, rope_n, gq_t, gk_n)


def _flash_update(k, q_pad, v_t, m_ref, l_ref, acc_ref, i, mask=None):
    s = jnp.dot(k, q_pad, preferred_element_type=F32)
    if mask is not None:
        s = jnp.where(mask, s, NEG)
    m_prev = m_ref[i]
    m_new = jnp.maximum(m_prev, jnp.max(s, axis=0, keepdims=True))
    alpha = jnp.exp(m_prev - m_new)
    p = jnp.exp(s - m_new)
    l_ref[i] = alpha * l_ref[i] + jnp.sum(p, axis=0, keepdims=True)
    acc_ref[i] = alpha * acc_ref[i] + jnp.dot(v_t, p.astype(BF16), preferred_element_type=F32)
    m_ref[i] = m_new


def _pad_q(q_head, slot):
    z = jnp.zeros_like(q_head)
    return jnp.concatenate([q_head, z] if slot == 0 else [z, q_head], axis=0)


def _chunk_range(qi, nl, nt):
    return jnp.where(qi >= nl, nl, 0), nt


def _attn_a_kernel(nl, nt, lam_init, q_ref, k_ref, v_ref, lam_ref, gs_ref, o_ref,
                   m_ref, l_ref, acc_ref):
    qi = pl.program_id(2)
    q = q_ref[0, 0]
    q_pads = [_pad_q(q[:HEAD_DIM], 0), _pad_q(q[HEAD_DIM:], 1)]
    m_ref[...] = jnp.full(m_ref.shape, NEG, F32)
    l_ref[...] = jnp.zeros(l_ref.shape, F32)
    acc_ref[...] = jnp.zeros(acc_ref.shape, F32)

    def body(j, carry):
        k = k_ref[0, pl.ds(pl.multiple_of(j * CH, CH), CH), :]
        v_t = v_ref[0, j]
        for i in range(2):
            _flash_update(k, q_pads[i], v_t, m_ref, l_ref, acc_ref, i)
        return carry

    lo, hi = _chunk_range(qi, nl, nt)
    lax.fori_loop(lo, hi, body, 0)

    lp = lam_ref[...]
    lam = (jnp.exp(jnp.sum(lp[0:1] * lp[1:2], axis=-1, keepdims=True))
           - jnp.exp(jnp.sum(lp[2:3] * lp[3:4], axis=-1, keepdims=True)) + lam_init)
    o = acc_ref[0] / l_ref[0] - lam * (acc_ref[1] / l_ref[1])
    o = o * lax.rsqrt(jnp.mean(o * o, axis=0, keepdims=True) + SUBLN_EPS) * gs_ref[...]
    o_ref[0] = (o * (1.0 - lam_init)).T.astype(BF16)


def _attn_a_call(q_t, k, v_t, lam_p, gs_t, nl, nq, lam_init):
    nb, nt = q_t.shape[:2]
    t = k.shape[1]
    ew = 2 * HEAD_DIM
    return pl.pallas_call(
        functools.partial(_attn_a_kernel, nl, nt, lam_init),
        grid=(nb, A_HEADS, nq),
        in_specs=[pl.BlockSpec((1, 1, ew, CH), lambda b, h, i: (b, i, h, 0)),
                  pl.BlockSpec((1, t, KV_PAIR), lambda b, h, i: (b, 0, h)),
                  pl.BlockSpec((1, nt, ew, CH), lambda b, h, i: (b, 0, h, 0)),
                  pl.BlockSpec(lam_p.shape, lambda b, h, i: (0, 0)),
                  pl.BlockSpec(gs_t.shape, lambda b, h, i: (0, 0))],
        out_specs=pl.BlockSpec((1, CH, ew), lambda b, h, i: (b, i, h)),
        out_shape=jax.ShapeDtypeStruct((nb, nq * CH, A_W), BF16),
        scratch_shapes=[pltpu.VMEM((2, 1, CH), F32), pltpu.VMEM((2, 1, CH), F32),
                        pltpu.VMEM((2, ew, CH), F32)],
        compiler_params=_cparams("parallel", "parallel", "arbitrary"),
        name="attn_differential",
    )(q_t, k, v_t, lam_p, gs_t)


def _attn_b_kernel(nl, nt, q_ref, k_ref, v_ref, o_ref, m_ref, l_ref, acc_ref):
    n = pl.program_id(1)
    qi = pl.program_id(2)
    group = B_HEADS // B_KV
    q = q_ref[0, 0]
    q_pads = []
    for h in range(group):
        qh = q[h * HEAD_DIM:(h + 1) * HEAD_DIM]
        q_pads.append(jnp.where(n == 0, _pad_q(qh, 0), _pad_q(qh, 1)))
    m_ref[...] = jnp.full(m_ref.shape, NEG, F32)
    l_ref[...] = jnp.zeros(l_ref.shape, F32)
    acc_ref[...] = jnp.zeros(acc_ref.shape, F32)

    def body(j, carry):
        k = k_ref[0, pl.ds(pl.multiple_of(j * CH, CH), CH), :]
        v_t = v_ref[0, j]
        for h in range(group):
            _flash_update(k, q_pads[h], v_t, m_ref, l_ref, acc_ref, h)
        return carry

    lo, hi = _chunk_range(qi, nl, nt)
    lax.fori_loop(lo, hi, body, 0)
    o = jnp.concatenate([acc_ref[h] / l_ref[h] for h in range(group)], axis=0)
    o_ref[0] = o.T.astype(BF16)


def _attn_b_call(q_t, k, v_t, nl, nq):
    nb, nt = q_t.shape[:2]
    t = k.shape[1]
    group = B_HEADS // B_KV
    gw = group * HEAD_DIM
    return pl.pallas_call(
        functools.partial(_attn_b_kernel, nl, nt),
        grid=(nb, B_KV, nq),
        in_specs=[pl.BlockSpec((1, 1, gw, CH), lambda b, n, i: (b, i, n, 0)),
                  pl.BlockSpec((1, t, KV_PAIR), lambda b, n, i: (b, 0, 0)),
                  pl.BlockSpec((1, nt, HEAD_DIM, CH), lambda b, n, i: (b, 0, n, 0))],
        out_specs=pl.BlockSpec((1, CH, gw), lambda b, n, i: (b, i, n)),
        out_shape=jax.ShapeDtypeStruct((nb, nq * CH, B_W), BF16),
        scratch_shapes=[pltpu.VMEM((group, 1, CH), F32), pltpu.VMEM((group, 1, CH), F32),
                        pltpu.VMEM((group, HEAD_DIM, CH), F32)],
        compiler_params=_cparams("parallel", "parallel", "arbitrary"),
        name="attn_dense_gqa",
    )(q_t, k, v_t)


def _attn_c_kernel(nl, sink_ref, q_ref, k0_ref, k1_ref, k2_ref, kx_ref,
                   v0_ref, v1_ref, v2_ref, vx_ref, o_ref, m_ref, l_ref, acc_ref):
    qi = pl.program_id(1)
    group = C_HEADS // C_KV
    q = q_ref[0, 0]
    for h in range(C_HEADS):
        m_ref[h] = jnp.full((1, CH), sink_ref[h], F32)
    l_ref[...] = jnp.ones(l_ref.shape, F32)
    acc_ref[...] = jnp.zeros(acc_ref.shape, F32)

    dist = (lax.broadcasted_iota(jnp.int32, (CH, CH), 0)
            - lax.broadcasted_iota(jnp.int32, (CH, CH), 1))
    chunks = []
    for c, (k_ref, v_ref) in enumerate(((k0_ref, v0_ref), (k1_ref, v1_ref), (k2_ref, v2_ref))):
        kj = qi + (c - 1)
        live = (kj >= 0) & (kj < nl) & (qi < nl)
        shift = jnp.where(live, (c - 1) * CH, 4 * CH)
        mask = jnp.abs(dist + shift) <= WINDOW
        chunks.append((k_ref[0], v_ref[0, 0], mask))
    chunks.append((kx_ref[0], vx_ref[0, 0], None))

    for k, v_t, mask in chunks:
        for h in range(C_HEADS):
            n = h // group
            q_pad = _pad_q(q[h * HEAD_DIM:(h + 1) * HEAD_DIM], n)
            _flash_update(k, q_pad, v_t[n * HEAD_DIM:(n + 1) * HEAD_DIM], m_ref, l_ref, acc_ref,
                          h, mask)
    o = jnp.concatenate([acc_ref[h] / l_ref[h] for h in range(C_HEADS)], axis=0)
    o_ref[0] = o.T.astype(BF16)


def _attn_c_call(sink, q_t, k, v_t, nl, nq):
    nb, nt = q_t.shape[:2]
    t = k.shape[1]
    band = lambda c: (lambda b, i: (b, jnp.clip(i + (c - 1), 0, nl - 1), 0))
    band_t = lambda c: (lambda b, i: (b, jnp.clip(i + (c - 1), 0, nl - 1), 0, 0))
    k_spec = lambda f: pl.BlockSpec((1, CH, KV_PAIR), f)
    v_spec = lambda f: pl.BlockSpec((1, 1, C_KW, CH), f)
    return pl.pallas_call(
        functools.partial(_attn_c_kernel, nl),
        grid=(nb, nq),
        in_specs=[pl.BlockSpec(memory_space=pltpu.SMEM),
                  pl.BlockSpec((1, 1, C_W, CH), lambda b, i: (b, i, 0, 0)),
                  k_spec(band(0)), k_spec(band(1)), k_spec(band(2)),
                  k_spec(lambda b, i: (b, nl, 0)),
                  v_spec(band_t(0)), v_spec(band_t(1)), v_spec(band_t(2)),
                  v_spec(lambda b, i: (b, nl, 0, 0))],
        out_specs=pl.BlockSpec((1, CH, C_W), lambda b, i: (b, i, 0)),
        out_shape=jax.ShapeDtypeStruct((nb, nq * CH, C_W), BF16),
        scratch_shapes=[pltpu.VMEM((C_HEADS, 1, CH), F32), pltpu.VMEM((C_HEADS, 1, CH), F32),
                        pltpu.VMEM((C_HEADS, HEAD_DIM, CH), F32)],
        compiler_params=_cparams("parallel", "parallel"),
        name="attn_window",
    )(sink, q_t, k, k, k, k, v_t, v_t, v_t, v_t)


def _merge_kernel(z_ref, mod_ref, gpre_ref, gpost_ref, oa_ref, ob_ref, oc_ref,
                  wg_ref, wbr_ref, wo_ref, out_ref):
    x = z_ref[0]
    m = mod_ref[...]
    d = x.shape[-1]
    xb = _modulate(x, gpre_ref[...], m[0:1], m[1:2]).astype(BF16)
    mixed = None
    row = 0
    for i, o_ref in enumerate((oa_ref, ob_ref, oc_ref)):
        w = o_ref.shape[-1]
        gate = jax.nn.sigmoid(jnp.dot(xb, wg_ref[:, i * d:(i + 1) * d], preferred_element_type=F32))
        term = gate * jnp.dot(o_ref[0], wbr_ref[row:row + w, :], preferred_element_type=F32)
        mixed = term if mixed is None else mixed + term
        row += w
    mix = jnp.dot(mixed.astype(BF16), wo_ref[...], preferred_element_type=F32)
    out_ref[0] = x + m[2:3] * _rms(mix, gpost_ref[...], EPS)


def _merge_call(z, mod, g_pre, g_post, oa, ob, oc, wg, wbr, wo, nl, nq):
    nb, _, d = z.shape
    tok = lambda w: pl.BlockSpec((1, CH, w), lambda b, j: (b, j, 0))
    return pl.pallas_call(
        _merge_kernel,
        grid=(nb, nq),
        in_specs=[tok(d), pl.BlockSpec((None, 6, d), _mod_index(nl, nb)),
                  _const_spec((1, d)), _const_spec((1, d)),
                  tok(A_W), tok(B_W), tok(C_W),
                  _const_spec(wg.shape), _const_spec(wbr.shape), _const_spec(wo.shape)],
        out_specs=tok(d),
        out_shape=jax.ShapeDtypeStruct((nb, nq * CH, d), F32),
        compiler_params=_cparams("parallel", "parallel"),
        name="merge_branches",
    )(z, mod, g_pre, g_post, oa, ob, oc, wg, wbr, wo)


def _mlp_kernel(z_ref, mod_ref, gpre_ref, gpost_ref, w1_ref, w2_ref, out_ref):
    x = z_ref[0]
    m = mod_ref[...]
    xb = _modulate(x, gpre_ref[...], m[3:4], m[4:5]).astype(BF16)
    h = jnp.maximum(jnp.dot(xb, w1_ref[...], preferred_element_type=F32), 0.0)
    y = jnp.dot((h * h).astype(BF16), w2_ref[...], preferred_element_type=F32)
    out_ref[0] = x + m[5:6] * _rms(y, gpost_ref[...], EPS)


def _mlp_call(z, mod, g_pre, g_post, w1, w2, nl):
    nb, t, d = z.shape
    tok = pl.BlockSpec((1, CH, d), lambda b, j: (b, j, 0))
    return pl.pallas_call(
        _mlp_kernel,
        grid=(nb, t // CH),
        in_specs=[tok, pl.BlockSpec((None, 6, d), _mod_index(nl, nb)),
                  _const_spec((1, d)), _const_spec((1, d)),
                  _const_spec(w1.shape), _const_spec(w2.shape)],
        out_specs=tok,
        out_shape=jax.ShapeDtypeStruct((nb, t, d), F32),
        compiler_params=_cparams("parallel", "parallel"),
        name="sq_relu_mlp",
    )(z, mod, g_pre, g_post, w1, w2)


def _rope_tables(seq, ctx_len):
    pos = jnp.arange(seq, dtype=jnp.int32)
    row = (pos // GRID_W).astype(F32)
    col = (pos % GRID_W).astype(F32)
    half = HEAD_DIM // 2
    inv_freq = jnp.power(ROPE_THETA, -jnp.arange(0, half, 2, dtype=F32) / half)
    ang_r = row[:, None] * inv_freq[None, :]
    ang_c = col[:, None] * inv_freq[None, :]
    cos = jnp.concatenate([jnp.cos(ang_r)] * 2 + [jnp.cos(ang_c)] * 2, axis=-1)
    sin = jnp.concatenate([-jnp.sin(ang_r), jnp.sin(ang_r), -jnp.sin(ang_c), jnp.sin(ang_c)], axis=-1)
    cos = jnp.concatenate([cos, jnp.ones((ctx_len, HEAD_DIM), F32)], axis=0)
    sin = jnp.concatenate([sin, jnp.zeros((ctx_len, HEAD_DIM), F32)], axis=0)
    nt = (seq + ctx_len) // CH
    to_t = lambda a: a.reshape(nt, CH, HEAD_DIM).transpose(0, 2, 1)
    rope_t = jnp.stack([to_t(cos), to_t(sin)], axis=1)
    lower = (jnp.arange(HEAD_DIM) % (half)) < (half // 2)
    pair = lambda a: jnp.concatenate([a, a], axis=-1)
    rope_n = jnp.stack([pair(cos), pair(jnp.where(lower, sin, 0.0)),
                        pair(jnp.where(lower, 0.0, sin))], axis=0)
    return rope_t, rope_n


def kernel(x, c, ctx, c_ctx, w_ada, b_ada, g_pre_mix, g_post_mix, g_pre_ff, g_post_ff, w_in, g_qnorm,
           g_knorm, lam_q1, lam_k1, lam_q2, lam_k2, g_subln, sink, w_branch, w_out, w_ff1, w_ff2):
    nb, seq, d = x.shape
    ctx_len = ctx.shape[1]
    depth = w_ada.shape[0]
    assert seq % CH == 0 and ctx_len == CH and nb + 1 <= MOD_ROWS and seq % GRID_W == 0
    nl = seq // CH
    nt = nl + 1

    z = jnp.concatenate([x, ctx], axis=1)
    cvec = jnp.zeros((MOD_ROWS, d), F32).at[:nb].set(c).at[nb].set(c_ctx)
    mods = _mod_call(cvec, w_ada, b_ada).reshape(depth, MOD_ROWS, 6, d)
    rope_t, rope_n = _rope_tables(seq, ctx_len)

    splits = (A_W, A_W, A_W, B_W, B_KW, B_KW, C_W, C_KW, C_KW)
    offs = [0]
    for s in splits:
        offs.append(offs[-1] + s)
    col = lambda w, i: w[:, offs[i]:offs[i + 1]]

    for l in range(depth):
        last = l == depth - 1
        nq = nl if last else nt
        lam_init = 0.8 - 0.6 * math.exp(-0.3 * l)
        w = w_in[l]
        wt = jnp.concatenate([col(w, i) for i in (0, 2, 3, 5, 6, 8)], axis=1).T.astype(BF16)
        wn = jnp.concatenate([col(w, i) for i in (1, 4, 7)], axis=1).astype(BF16)
        wg = w[:, offs[-1]:].astype(BF16)
        row = lambda g: g[l].reshape(1, -1)
        gq_t = jnp.broadcast_to(g_qnorm[l][:, None], (HEAD_DIM, CH))
        gk_n = jnp.concatenate([g_knorm[l]] * B_KV).reshape(1, KV_PAIR)
        gs_t = jnp.broadcast_to(g_subln[l][:, None], (2 * HEAD_DIM, CH))
        lam_p = jnp.stack([lam_q1[l], lam_k1[l], lam_q2[l], lam_k2[l]], axis=0)

        qa, va, qb, vb, qc, vc, ka, kb, kc = _inproj_call(
            z, mods[l], row(g_pre_mix), wt, wn, rope_t, rope_n, gq_t, gk_n, nl)
        oa = _attn_a_call(qa, ka, va, lam_p, gs_t, nl, nq, lam_init)
        ob = _attn_b_call(qb, kb, vb, nl, nq)
        oc = _attn_c_call(sink[l], qc, kc, vc, nl, nq)
        z = _merge_call(z, mods[l], row(g_pre_mix), row(g_post_mix), oa, ob, oc, wg,
                        w_branch[l].astype(BF16), w_out[l].astype(BF16), nl, nq)
        z = _mlp_call(z, mods[l], row(g_pre_ff), row(g_post_ff),
                      w_ff1[l].astype(BF16), w_ff2[l].astype(BF16), nl)
    return z
```

```python
import functools
import math

import jax
import jax.numpy as jnp
from jax import lax
from jax.experimental import pallas as pl
from jax.experimental.pallas import tpu as pltpu

HEAD_DIM = 64
A_HEADS = 4
B_HEADS = 8
B_KV = 2
C_HEADS = 8
C_KV = 2
WINDOW = 128
GRID_W = 64
ROPE_THETA = 10000.0
EPS = 1e-6
SUBLN_EPS = 1e-5

A_W = A_HEADS * 2 * HEAD_DIM
B_W = B_HEADS * HEAD_DIM
C_W = C_HEADS * HEAD_DIM
B_KW = B_KV * HEAD_DIM
C_KW = C_KV * HEAD_DIM
KV_PAIR = 2 * HEAD_DIM

CH = 256
KB = 2
NBUF = 4
MOD_ROWS = 8
NEG = -1e30
LOG2E = math.log2(math.e)
DEN_ROWS = 16
VMEM_LIMIT = 48 * 1024 * 1024

F32 = jnp.float32
BF16 = jnp.bfloat16


def _cparams(*sem):
    return pltpu.CompilerParams(dimension_semantics=sem, vmem_limit_bytes=VMEM_LIMIT)


def _const_spec(shape):
    nd = len(shape)
    return pl.BlockSpec(shape, lambda *_: (0,) * nd, pipeline_mode=pl.Buffered(1))


def _rms(x, gain, eps):
    return x * lax.rsqrt(jnp.mean(x * x, axis=-1, keepdims=True) + eps) * gain


def _modulate(x, gain, shift, scale):
    return _rms(x, gain, EPS) * (1.0 + scale) + shift


def _mod_kernel(c_ref, w_ref, b_ref, o_ref):
    c = c_ref[...]
    s = c * jax.nn.sigmoid(c)
    o_ref[0] = jnp.dot(s, w_ref[0], preferred_element_type=F32,
                       precision=lax.Precision.HIGHEST) + b_ref[0]


def _mod_call(cvec, w_ada, b_ada):
    depth, d, n = w_ada.shape
    tn = n // 4
    return pl.pallas_call(
        _mod_kernel,
        grid=(depth, n // tn),
        in_specs=[pl.BlockSpec((MOD_ROWS, d), lambda l, j: (0, 0)),
                  pl.BlockSpec((1, d, tn), lambda l, j: (l, 0, j)),
                  pl.BlockSpec((1, 1, tn), lambda l, j: (l, 0, j))],
        out_specs=pl.BlockSpec((1, MOD_ROWS, tn), lambda l, j: (l, 0, j)),
        out_shape=jax.ShapeDtypeStruct((depth, MOD_ROWS, n), F32),
        compiler_params=_cparams("parallel", "parallel"),
        name="adaln_vectors",
    )(cvec, w_ada, b_ada.reshape(depth, 1, n))


def _rope_t(y3, cos_t, sin_t):
    part = jnp.concatenate([y3[:, 16:32], y3[:, 0:16], y3[:, 48:64], y3[:, 32:48]], axis=1)
    return y3 * cos_t[None] + part * sin_t[None]


def _rope_n(y, cos_n, sin_lo, sin_hi):
    return (y * cos_n + pltpu.roll(y, KV_PAIR - 16, axis=1) * sin_lo
            + pltpu.roll(y, 16, axis=1) * sin_hi)


def _inproj_kernel(z_ref, mod_ref, g_ref, wt_ref, wn_ref, rt_ref, rn_ref, gq_ref, gk_ref,
                   qa_ref, va_ref, qb_ref, vb_ref, qc_ref, vc_ref, ka_ref, kb_ref, kc_ref):
    m = mod_ref[...]
    xb = _modulate(z_ref[0], g_ref[...], m[0:1], m[1:2]).astype(BF16)
    cos_t, sin_t = rt_ref[0, 0], rt_ref[0, 1]
    cos_n, sin_lo, sin_hi = rn_ref[0], rn_ref[1], rn_ref[2]
    ch = xb.shape[0]
    q_scale = HEAD_DIM ** -0.5 * LOG2E

    def proj_t(lo, hi):
        return lax.dot_general(wt_ref[lo:hi, :], xb, (((1,), (1,)), ((), ())),
                               preferred_element_type=F32)

    def store_q(ref, y3):
        ref[0, 0] = (_rope_t(y3, cos_t, sin_t) * q_scale).reshape(-1, ch).astype(BF16)

    o = 0
    store_q(qa_ref, proj_t(o, o + A_W).reshape(-1, HEAD_DIM, ch))
    o += A_W
    va_ref[0, 0] = proj_t(o, o + A_W).astype(BF16)
    o += A_W
    qb = proj_t(o, o + B_W).reshape(-1, HEAD_DIM, ch)
    qb = qb * lax.rsqrt(jnp.mean(qb * qb, axis=1, keepdims=True) + EPS) * gq_ref[...][None]
    store_q(qb_ref, qb)
    o += B_W
    vb_ref[0, 0] = proj_t(o, o + B_KW).astype(BF16)
    o += B_KW
    store_q(qc_ref, proj_t(o, o + C_W).reshape(-1, HEAD_DIM, ch))
    o += C_W
    vc_ref[0, 0] = proj_t(o, o + C_KW).astype(BF16)

    kn = jnp.dot(xb, wn_ref[...], preferred_element_type=F32)
    for i in range(A_W // KV_PAIR):
        ka_ref[0, :, i * KV_PAIR:(i + 1) * KV_PAIR] = _rope_n(
            kn[:, i * KV_PAIR:(i + 1) * KV_PAIR], cos_n, sin_lo, sin_hi).astype(BF16)
    kb = kn[:, A_W:A_W + B_KW]
    sq = kb * kb
    first = lax.broadcasted_iota(jnp.int32, sq.shape, 1) < HEAD_DIM
    s0 = jnp.sum(jnp.where(first, sq, 0.0), axis=-1, keepdims=True)
    s1 = jnp.sum(jnp.where(first, 0.0, sq), axis=-1, keepdims=True)
    ms = jnp.where(first, s0, s1) * (1.0 / HEAD_DIM)
    kb = kb * lax.rsqrt(ms + EPS) * gk_ref[...]
    kb_ref[0] = _rope_n(kb, cos_n, sin_lo, sin_hi).astype(BF16)
    kc_ref[0] = _rope_n(kn[:, A_W + B_KW:], cos_n, sin_lo, sin_hi).astype(BF16)


def _mod_index(nl, nb):
    return lambda b, j: (jnp.where(j < nl, b, nb), 0, 0)


def _inproj_call(z, mod, g_pre, wt, wn, rope_t, rope_n, gq_t, gk_n, nl):
    nb, t, d = z.shape
    nt = t // CH
    tok = lambda w: pl.BlockSpec((1, 1, w, CH), lambda b, j: (b, j, 0, 0))
    nat = lambda w: pl.BlockSpec((1, CH, w), lambda b, j: (b, j, 0))
    tshape = lambda w: jax.ShapeDtypeStruct((nb, nt, w, CH), BF16)
    nshape = lambda w: jax.ShapeDtypeStruct((nb, t, w), BF16)
    return pl.pallas_call(
        _inproj_kernel,
        grid=(nb, nt),
        in_specs=[pl.BlockSpec((1, CH, d), lambda b, j: (b, j, 0)),
                  pl.BlockSpec((None, 6, d), _mod_index(nl, nb)),
                  _const_spec((1, d)),
                  _const_spec(wt.shape),
                  _const_spec(wn.shape),
                  pl.BlockSpec((1, 2, HEAD_DIM, CH), lambda b, j: (j, 0, 0, 0)),
                  pl.BlockSpec((3, CH, KV_PAIR), lambda b, j: (0, j, 0)),
                  _const_spec(gq_t.shape),
                  _const_spec(gk_n.shape)],
        out_specs=[tok(A_W), tok(A_W), tok(B_W), tok(B_KW), tok(C_W), tok(C_KW),
                   nat(A_W), nat(B_KW), nat(C_KW)],
        out_shape=[tshape(A_W), tshape(A_W), tshape(B_W), tshape(B_KW), tshape(C_W), tshape(C_KW),
                   nshape(A_W), nshape(B_KW), nshape(C_KW)],
        compiler_params=_cparams("parallel", "parallel"),
        name="in_projection",
    )(z, mod, g_pre, wt, wn, rope_t, rope_n, gq_t, gk_n)


def _with_ones(v_t):
    return jnp.concatenate([v_t, jnp.ones((DEN_ROWS, v_t.shape[1]), v_t.dtype)], axis=0)


def _flash_step(k, q_cat, v_t, m_ref, acc_ref, bias=None):
    s = jnp.dot(k, q_cat, preferred_element_type=F32)
    if bias is not None:
        s = s + bias
    _softmax_pv(s, v_t, m_ref, acc_ref)


def _softmax_pv(s, v_t, m_ref, acc_ref):
    m_prev = m_ref[...]
    m_new = jnp.maximum(m_prev, jnp.max(s, axis=0, keepdims=True))
    alpha = jnp.exp2(m_prev - m_new)
    p = jnp.exp2(s - m_new).astype(BF16)
    acc_ref[...] = alpha * acc_ref[...] + jnp.dot(_with_ones(v_t), p, preferred_element_type=F32)
    m_ref[...] = m_new


def _normalized(acc_ref):
    e = acc_ref.shape[0] - DEN_ROWS
    return acc_ref[:e, :] / acc_ref[e:e + 1, :]


def _cat_queries(q, heads, slot):
    parts = []
    for h in range(heads):
        qh = q[h * HEAD_DIM:(h + 1) * HEAD_DIM]
        z = jnp.zeros_like(qh)
        parts.append(jnp.concatenate([qh, z] if slot(h) == 0 else [z, qh], axis=0))
    return jnp.concatenate(parts, axis=1)


def _sweep_keys(is_latent, nl, k_ref, v_ref, q_cat, m_ref, acc_ref, s_ref):
    m_ref[...] = jnp.full(m_ref.shape, NEG, F32)
    acc_ref[...] = jnp.zeros(acc_ref.shape, F32)
    nblk = nl // KB

    def scores(b):
        k = k_ref[0, pl.ds(pl.multiple_of(b * (KB * CH), KB * CH), KB * CH), :]
        return jnp.dot(k, q_cat, preferred_element_type=F32)

    def values(b):
        return jnp.concatenate([v_ref[0, KB * b + c] for c in range(KB)], axis=1)

    s_ref[0] = scores(0)
    _flash_step(k_ref[0, nl * CH:(nl + 1) * CH, :], q_cat, v_ref[0, nl], m_ref, acc_ref)

    def body(i, carry):
        b = NBUF * i
        for u in range(NBUF):
            s_ref[(u + 1) % NBUF] = scores(jnp.minimum(b + u + 1, nblk - 1))
            _softmax_pv(s_ref[u], values(b + u), m_ref, acc_ref)
        return carry

    lax.fori_loop(0, jnp.where(is_latent, nblk // NBUF, 0), body, 0)


def _attn_a_kernel(nl, lam_init, q_ref, k_ref, v_ref, lam_ref, gs_ref, o_ref,
                   m_ref, acc_ref, s_ref):
    q_cat = _cat_queries(q_ref[0, 0], 2, lambda h: h)
    _sweep_keys(pl.program_id(2) < nl, nl, k_ref, v_ref, q_cat, m_ref, acc_ref, s_ref)

    lp = lam_ref[...]
    lam = (jnp.exp(jnp.sum(lp[0:1] * lp[1:2], axis=-1, keepdims=True))
           - jnp.exp(jnp.sum(lp[2:3] * lp[3:4], axis=-1, keepdims=True)) + lam_init)
    o = _normalized(acc_ref)
    o = o[:, :CH] - lam * o[:, CH:]
    o = o * lax.rsqrt(jnp.mean(o * o, axis=0, keepdims=True) + SUBLN_EPS) * gs_ref[...]
    o_ref[0] = (o * (1.0 - lam_init)).T.astype(BF16)


def _attn_a_call(q_t, k, v_t, lam_p, gs_t, nl, nq, lam_init):
    nb, nt = q_t.shape[:2]
    t = k.shape[1]
    ew = 2 * HEAD_DIM
    return pl.pallas_call(
        functools.partial(_attn_a_kernel, nl, lam_init),
        grid=(nb, A_HEADS, nq),
        in_specs=[pl.BlockSpec((1, 1, ew, CH), lambda b, h, i: (b, i, h, 0)),
                  pl.BlockSpec((1, t, KV_PAIR), lambda b, h, i: (b, 0, h)),
                  pl.BlockSpec((1, nt, ew, CH), lambda b, h, i: (b, 0, h, 0)),
                  pl.BlockSpec(lam_p.shape, lambda b, h, i: (0, 0)),
                  pl.BlockSpec(gs_t.shape, lambda b, h, i: (0, 0))],
        out_specs=pl.BlockSpec((1, CH, ew), lambda b, h, i: (b, i, h)),
        out_shape=jax.ShapeDtypeStruct((nb, nq * CH, A_W), BF16),
        scratch_shapes=[pltpu.VMEM((1, 2 * CH), F32), pltpu.VMEM((ew + DEN_ROWS, 2 * CH), F32),
                        pltpu.VMEM((NBUF, KB * CH, 2 * CH), F32)],
        compiler_params=_cparams("parallel", "parallel", "arbitrary"),
        name="attn_differential",
    )(q_t, k, v_t, lam_p, gs_t)


def _heads_to_rows(o, heads):
    return jnp.concatenate([o[:, h * CH:(h + 1) * CH] for h in range(heads)], axis=0)


def _attn_b_kernel(nl, q_ref, k_ref, v_ref, o_ref, m_ref, acc_ref, s_ref):
    n = pl.program_id(1)
    group = B_HEADS // B_KV
    q = q_ref[0, 0]
    q_cat = jnp.where(n == 0, _cat_queries(q, group, lambda h: 0),
                      _cat_queries(q, group, lambda h: 1))
    _sweep_keys(pl.program_id(2) < nl, nl, k_ref, v_ref, q_cat, m_ref, acc_ref, s_ref)
    o_ref[0] = _heads_to_rows(_normalized(acc_ref), group).T.astype(BF16)


def _attn_b_call(q_t, k, v_t, nl, nq):
    nb, nt = q_t.shape[:2]
    t = k.shape[1]
    group = B_HEADS // B_KV
    gw = group * HEAD_DIM
    return pl.pallas_call(
        functools.partial(_attn_b_kernel, nl),
        grid=(nb, B_KV, nq),
        in_specs=[pl.BlockSpec((1, 1, gw, CH), lambda b, n, i: (b, i, n, 0)),
                  pl.BlockSpec((1, t, KV_PAIR), lambda b, n, i: (b, 0, 0)),
                  pl.BlockSpec((1, nt, HEAD_DIM, CH), lambda b, n, i: (b, 0, n, 0))],
        out_specs=pl.BlockSpec((1, CH, gw), lambda b, n, i: (b, i, n)),
        out_shape=jax.ShapeDtypeStruct((nb, nq * CH, B_W), BF16),
        scratch_shapes=[pltpu.VMEM((1, group * CH), F32),
                        pltpu.VMEM((HEAD_DIM + DEN_ROWS, group * CH), F32),
                        pltpu.VMEM((NBUF, KB * CH, group * CH), F32)],
        compiler_params=_cparams("parallel", "parallel", "arbitrary"),
        name="attn_dense_gqa",
    )(q_t, k, v_t)


def _attn_c_kernel(nl, sink_ref, q_ref, bias_ref, k0_ref, k1_ref, k2_ref, kx_ref,
                   v0_ref, v1_ref, v2_ref, vx_ref, o_ref, m_ref, acc_ref):
    qi = pl.program_id(1)
    group = C_HEADS // C_KV
    gw = group * HEAD_DIM
    q = q_ref[0, 0]
    band = ((k0_ref, v0_ref), (k1_ref, v1_ref), (k2_ref, v2_ref))
    for n in range(C_KV):
        rows = slice(n * HEAD_DIM, (n + 1) * HEAD_DIM)
        q_cat = _cat_queries(q[n * gw:(n + 1) * gw], group, lambda h: n)
        m_ref[...] = jnp.concatenate(
            [jnp.full((1, CH), sink_ref[n * group + h] * LOG2E, F32) for h in range(group)], axis=1)
        acc_ref[:HEAD_DIM, :] = jnp.zeros((HEAD_DIM, group * CH), F32)
        acc_ref[HEAD_DIM:, :] = jnp.ones((DEN_ROWS, group * CH), F32)
        for c, (k_ref, v_ref) in enumerate(band):
            kj = qi + (c - 1)

            @pl.when((kj >= 0) & (kj < nl) & (qi < nl))
            def _(c=c, k_ref=k_ref, v_ref=v_ref):
                _flash_step(k_ref[0], q_cat, v_ref[0, 0, rows], m_ref, acc_ref, bias_ref[c])

        _flash_step(kx_ref[0], q_cat, vx_ref[0, 0, rows], m_ref, acc_ref)
        o_ref[0, :, n * gw:(n + 1) * gw] = _heads_to_rows(
            _normalized(acc_ref), group).T.astype(BF16)


def _window_bias(group):
    key = jnp.arange(CH, dtype=jnp.int32)[:, None]
    qry = jnp.arange(CH, dtype=jnp.int32)[None, :]
    tiles = [jnp.where(jnp.abs(key - qry + (c - 1) * CH) <= WINDOW, 0.0, NEG).astype(F32)
             for c in range(3)]
    return jnp.stack([jnp.concatenate([t] * group, axis=1) for t in tiles], axis=0)


def _attn_c_call(sink, q_t, k, v_t, nl, nq):
    nb, nt = q_t.shape[:2]
    t = k.shape[1]
    group = C_HEADS // C_KV
    bias = _window_bias(group)
    band = lambda c: (lambda b, i: (b, jnp.clip(i + (c - 1), 0, nl - 1), 0))
    band_t = lambda c: (lambda b, i: (b, jnp.clip(i + (c - 1), 0, nl - 1), 0, 0))
    k_spec = lambda f: pl.BlockSpec((1, CH, KV_PAIR), f)
    v_spec = lambda f: pl.BlockSpec((1, 1, C_KW, CH), f)
    return pl.pallas_call(
        functools.partial(_attn_c_kernel, nl),
        grid=(nb, nq),
        in_specs=[pl.BlockSpec(memory_space=pltpu.SMEM),
                  pl.BlockSpec((1, 1, C_W, CH), lambda b, i: (b, i, 0, 0)),
                  _const_spec(bias.shape),
                  k_spec(band(0)), k_spec(band(1)), k_spec(band(2)),
                  k_spec(lambda b, i: (b, nl, 0)),
                  v_spec(band_t(0)), v_spec(band_t(1)), v_spec(band_t(2)),
                  v_spec(lambda b, i: (b, nl, 0, 0))],
        out_specs=pl.BlockSpec((1, CH, C_W), lambda b, i: (b, i, 0)),
        out_shape=jax.ShapeDtypeStruct((nb, nq * CH, C_W), BF16),
        scratch_shapes=[pltpu.VMEM((1, group * CH), F32),
                        pltpu.VMEM((HEAD_DIM + DEN_ROWS, group * CH), F32)],
        compiler_params=_cparams("parallel", "parallel"),
        name="attn_window",
    )(sink, q_t, bias, k, k, k, k, v_t, v_t, v_t, v_t)


def _merge_kernel(z_ref, mod_ref, gpre_ref, gpost_ref, oa_ref, ob_ref, oc_ref,
                  wg_ref, wbr_ref, wo_ref, out_ref):
    x = z_ref[0]
    m = mod_ref[...]
    d = x.shape[-1]
    xb = _modulate(x, gpre_ref[...], m[0:1], m[1:2]).astype(BF16)
    mixed = None
    row = 0
    for i, o_ref in enumerate((oa_ref, ob_ref, oc_ref)):
        w = o_ref.shape[-1]
        gate = jax.nn.sigmoid(jnp.dot(xb, wg_ref[:, i * d:(i + 1) * d], preferred_element_type=F32))
        term = gate * jnp.dot(o_ref[0], wbr_ref[row:row + w, :], preferred_element_type=F32)
        mixed = term if mixed is None else mixed + term
        row += w
    mix = jnp.dot(mixed.astype(BF16), wo_ref[...], preferred_element_type=F32)
    out_ref[0] = x + m[2:3] * _rms(mix, gpost_ref[...], EPS)


def _merge_call(z, mod, g_pre, g_post, oa, ob, oc, wg, wbr, wo, nl, nq):
    nb, _, d = z.shape
    tok = lambda w: pl.BlockSpec((1, CH, w), lambda b, j: (b, j, 0))
    return pl.pallas_call(
        _merge_kernel,
        grid=(nb, nq),
        in_specs=[tok(d), pl.BlockSpec((None, 6, d), _mod_index(nl, nb)),
                  _const_spec((1, d)), _const_spec((1, d)),
                  tok(A_W), tok(B_W), tok(C_W),
                  _const_spec(wg.shape), _const_spec(wbr.shape), _const_spec(wo.shape)],
        out_specs=tok(d),
        out_shape=jax.ShapeDtypeStruct((nb, nq * CH, d), F32),
        compiler_params=_cparams("parallel", "parallel"),
        name="merge_branches",
    )(z, mod, g_pre, g_post, oa, ob, oc, wg, wbr, wo)


def _mlp_kernel(z_ref, mod_ref, gpre_ref, gpost_ref, w1_ref, w2_ref, out_ref):
    x = z_ref[0]
    m = mod_ref[...]
    xb = _modulate(x, gpre_ref[...], m[3:4], m[4:5]).astype(BF16)
    h = jnp.maximum(jnp.dot(xb, w1_ref[...], preferred_element_type=F32), 0.0)
    y = jnp.dot((h * h).astype(BF16), w2_ref[...], preferred_element_type=F32)
    out_ref[0] = x + m[5:6] * _rms(y, gpost_ref[...], EPS)


def _mlp_call(z, mod, g_pre, g_post, w1, w2, nl):
    nb, t, d = z.shape
    tok = pl.BlockSpec((1, CH, d), lambda b, j: (b, j, 0))
    return pl.pallas_call(
        _mlp_kernel,
        grid=(nb, t // CH),
        in_specs=[tok, pl.BlockSpec((None, 6, d), _mod_index(nl, nb)),
                  _const_spec((1, d)), _const_spec((1, d)),
                  _const_spec(w1.shape), _const_spec(w2.shape)],
        out_specs=tok,
        out_shape=jax.ShapeDtypeStruct((nb, t, d), F32),
        compiler_params=_cparams("parallel", "parallel"),
        name="sq_relu_mlp",
    )(z, mod, g_pre, g_post, w1, w2)


def _rope_tables(seq, ctx_len):
    pos = jnp.arange(seq, dtype=jnp.int32)
    row = (pos // GRID_W).astype(F32)
    col = (pos % GRID_W).astype(F32)
    half = HEAD_DIM // 2
    inv_freq = jnp.power(ROPE_THETA, -jnp.arange(0, half, 2, dtype=F32) / half)
    ang_r = row[:, None] * inv_freq[None, :]
    ang_c = col[:, None] * inv_freq[None, :]
    cos = jnp.concatenate([jnp.cos(ang_r)] * 2 + [jnp.cos(ang_c)] * 2, axis=-1)
    sin = jnp.concatenate([-jnp.sin(ang_r), jnp.sin(ang_r), -jnp.sin(ang_c), jnp.sin(ang_c)], axis=-1)
    cos = jnp.concatenate([cos, jnp.ones((ctx_len, HEAD_DIM), F32)], axis=0)
    sin = jnp.concatenate([sin, jnp.zeros((ctx_len, HEAD_DIM), F32)], axis=0)
    nt = (seq + ctx_len) // CH
    to_t = lambda a: a.reshape(nt, CH, HEAD_DIM).transpose(0, 2, 1)
    rope_t = jnp.stack([to_t(cos), to_t(sin)], axis=1)
    lower = (jnp.arange(HEAD_DIM) % (half)) < (half // 2)
    pair = lambda a: jnp.concatenate([a, a], axis=-1)
    rope_n = jnp.stack([pair(cos), pair(jnp.where(lower, sin, 0.0)),
                        pair(jnp.where(lower, 0.0, sin))], axis=0)
    return rope_t, rope_n


def kernel(x, c, ctx, c_ctx, w_ada, b_ada, g_pre_mix, g_post_mix, g_pre_ff, g_post_ff, w_in, g_qnorm,
           g_knorm, lam_q1, lam_k1, lam_q2, lam_k2, g_subln, sink, w_branch, w_out, w_ff1, w_ff2):
    nb, seq, d = x.shape
    ctx_len = ctx.shape[1]
    depth = w_ada.shape[0]
    assert seq % (NBUF * KB * CH) == 0 and ctx_len == CH and nb + 1 <= MOD_ROWS and seq % GRID_W == 0
    nl = seq // CH
    nt = nl + 1

    z = jnp.concatenate([x, ctx], axis=1)
    cvec = jnp.zeros((MOD_ROWS, d), F32).at[:nb].set(c).at[nb].set(c_ctx)
    mods = _mod_call(cvec, w_ada, b_ada).reshape(depth, MOD_ROWS, 6, d)
    rope_t, rope_n = _rope_tables(seq, ctx_len)

    splits = (A_W, A_W, A_W, B_W, B_KW, B_KW, C_W, C_KW, C_KW)
    offs = [0]
    for s in splits:
        offs.append(offs[-1] + s)
    col = lambda w, i: w[:, offs[i]:offs[i + 1]]

    for l in range(depth):
        last = l == depth - 1
        nq = nl if last else nt
        lam_init = 0.8 - 0.6 * math.exp(-0.3 * l)
        w = w_in[l]
        wt = jnp.concatenate([col(w, i) for i in (0, 2, 3, 5, 6, 8)], axis=1).T.astype(BF16)
        wn = jnp.concatenate([col(w, i) for i in (1, 4, 7)], axis=1).astype(BF16)
        wg = w[:, offs[-1]:].astype(BF16)
        row = lambda g: g[l].reshape(1, -1)
        gq_t = jnp.broadcast_to(g_qnorm[l][:, None], (HEAD_DIM, CH))
        gk_n = jnp.concatenate([g_knorm[l]] * B_KV).reshape(1, KV_PAIR)
        gs_t = jnp.broadcast_to(g_subln[l][:, None], (2 * HEAD_DIM, CH))
        lam_p = jnp.stack([lam_q1[l], lam_k1[l], lam_q2[l], lam_k2[l]], axis=0)

        qa, va, qb, vb, qc, vc, ka, kb, kc = _inproj_call(
            z, mods[l], row(g_pre_mix), wt, wn, rope_t, rope_n, gq_t, gk_n, nl)
        oa = _attn_a_call(qa, ka, va, lam_p, gs_t, nl, nq, lam_init)
        ob = _attn_b_call(qb, kb, vb, nl, nq)
        oc = _attn_c_call(sink[l], qc, kc, vc, nl, nq)
        z = _merge_call(z, mods[l], row(g_pre_mix), row(g_post_mix), oa, ob, oc, wg,
                        w_branch[l].astype(BF16), w_out[l].astype(BF16), nl, nq)
        z = _mlp_call(z, mods[l], row(g_pre_ff), row(g_post_ff),
                      w_ff1[l].astype(BF16), w_ff2[l].astype(BF16), nl)
    return z
```

```python
import functools
import math

import jax
import jax.numpy as jnp
from jax import lax
from jax.experimental import pallas as pl
from jax.experimental.pallas import tpu as pltpu

HEAD_DIM = 64
A_HEADS = 4
B_HEADS = 8
B_KV = 2
C_HEADS = 8
C_KV = 2
WINDOW = 128
GRID_W = 64
ROPE_THETA = 10000.0
EPS = 1e-6
SUBLN_EPS = 1e-5

A_W = A_HEADS * 2 * HEAD_DIM
B_W = B_HEADS * HEAD_DIM
C_W = C_HEADS * HEAD_DIM
B_KW = B_KV * HEAD_DIM
C_KW = C_KV * HEAD_DIM
KV_PAIR = 2 * HEAD_DIM

CH = 256
KB = 2
NBUF = 8
A_STEP = 2
MOD_ROWS = 8
NEG = -1e30
LOG2E = math.log2(math.e)
DEN_ROWS = 16
VMEM_LIMIT = 48 * 1024 * 1024

F32 = jnp.float32
BF16 = jnp.bfloat16


def _cparams(*sem):
    return pltpu.CompilerParams(dimension_semantics=sem, vmem_limit_bytes=VMEM_LIMIT)


def _const_spec(shape):
    nd = len(shape)
    return pl.BlockSpec(shape, lambda *_: (0,) * nd, pipeline_mode=pl.Buffered(1))


def _rms(x, gain, eps):
    return x * lax.rsqrt(jnp.mean(x * x, axis=-1, keepdims=True) + eps) * gain


def _modulate(x, gain, shift, scale):
    return _rms(x, gain, EPS) * (1.0 + scale) + shift


def _mod_kernel(c_ref, w_ref, b_ref, o_ref):
    c = c_ref[...]
    s = c * jax.nn.sigmoid(c)
    o_ref[0] = jnp.dot(s, w_ref[0], preferred_element_type=F32,
                       precision=lax.Precision.HIGHEST) + b_ref[0]


def _mod_call(cvec, w_ada, b_ada):
    depth, d, n = w_ada.shape
    tn = n // 4
    return pl.pallas_call(
        _mod_kernel,
        grid=(depth, n // tn),
        in_specs=[pl.BlockSpec((MOD_ROWS, d), lambda l, j: (0, 0)),
                  pl.BlockSpec((1, d, tn), lambda l, j: (l, 0, j)),
                  pl.BlockSpec((1, 1, tn), lambda l, j: (l, 0, j))],
        out_specs=pl.BlockSpec((1, MOD_ROWS, tn), lambda l, j: (l, 0, j)),
        out_shape=jax.ShapeDtypeStruct((depth, MOD_ROWS, n), F32),
        compiler_params=_cparams("parallel", "parallel"),
        name="adaln_vectors",
    )(cvec, w_ada, b_ada.reshape(depth, 1, n))


def _rope_t(y3, cos_t, sin_t):
    part = jnp.concatenate([y3[:, 16:32], y3[:, 0:16], y3[:, 48:64], y3[:, 32:48]], axis=1)
    return y3 * cos_t[None] + part * sin_t[None]


def _rope_n(y, cos_n, sin_lo, sin_hi):
    return (y * cos_n + pltpu.roll(y, KV_PAIR - 16, axis=1) * sin_lo
            + pltpu.roll(y, 16, axis=1) * sin_hi)


def _inproj_kernel(z_ref, mod_ref, g_ref, wt_ref, wn_ref, rt_ref, rn_ref, gq_ref, gk_ref,
                   qa_ref, va_ref, qb_ref, vb_ref, qc_ref, vc_ref, ka_ref, kb_ref, kc_ref):
    m = mod_ref[...]
    xb = _modulate(z_ref[0], g_ref[...], m[0:1], m[1:2]).astype(BF16)
    cos_t, sin_t = rt_ref[0, 0], rt_ref[0, 1]
    cos_n, sin_lo, sin_hi = rn_ref[0], rn_ref[1], rn_ref[2]
    ch = xb.shape[0]
    q_scale = HEAD_DIM ** -0.5 * LOG2E

    def proj_t(lo, hi):
        return lax.dot_general(wt_ref[lo:hi, :], xb, (((1,), (1,)), ((), ())),
                               preferred_element_type=F32)

    def store_q(ref, y3):
        ref[0, 0] = (_rope_t(y3, cos_t, sin_t) * q_scale).reshape(-1, ch).astype(BF16)

    o = 0
    store_q(qa_ref, proj_t(o, o + A_W).reshape(-1, HEAD_DIM, ch))
    o += A_W
    va_ref[0, 0] = proj_t(o, o + A_W).astype(BF16)
    o += A_W
    qb = proj_t(o, o + B_W).reshape(-1, HEAD_DIM, ch)
    qb = qb * lax.rsqrt(jnp.mean(qb * qb, axis=1, keepdims=True) + EPS) * gq_ref[...][None]
    store_q(qb_ref, qb)
    o += B_W
    vb_ref[0, 0] = proj_t(o, o + B_KW).astype(BF16)
    o += B_KW
    store_q(qc_ref, proj_t(o, o + C_W).reshape(-1, HEAD_DIM, ch))
    o += C_W
    vc_ref[0, 0] = proj_t(o, o + C_KW).astype(BF16)

    kn = jnp.dot(xb, wn_ref[...], preferred_element_type=F32)
    for i in range(A_W // KV_PAIR):
        ka_ref[0, :, i * KV_PAIR:(i + 1) * KV_PAIR] = _rope_n(
            kn[:, i * KV_PAIR:(i + 1) * KV_PAIR], cos_n, sin_lo, sin_hi).astype(BF16)
    kb = kn[:, A_W:A_W + B_KW]
    sq = kb * kb
    first = lax.broadcasted_iota(jnp.int32, sq.shape, 1) < HEAD_DIM
    s0 = jnp.sum(jnp.where(first, sq, 0.0), axis=-1, keepdims=True)
    s1 = jnp.sum(jnp.where(first, 0.0, sq), axis=-1, keepdims=True)
    ms = jnp.where(first, s0, s1) * (1.0 / HEAD_DIM)
    kb = kb * lax.rsqrt(ms + EPS) * gk_ref[...]
    kb_ref[0] = _rope_n(kb, cos_n, sin_lo, sin_hi).astype(BF16)
    kc_ref[0] = _rope_n(kn[:, A_W + B_KW:], cos_n, sin_lo, sin_hi).astype(BF16)


def _mod_index(nl, nb):
    return lambda b, j: (jnp.where(j < nl, b, nb), 0, 0)


def _inproj_call(z, mod, g_pre, wt, wn, rope_t, rope_n, gq_t, gk_n, nl):
    nb, t, d = z.shape
    nt = t // CH
    tok = lambda w: pl.BlockSpec((1, 1, w, CH), lambda b, j: (b, j, 0, 0))
    nat = lambda w: pl.BlockSpec((1, CH, w), lambda b, j: (b, j, 0))
    tshape = lambda w: jax.ShapeDtypeStruct((nb, nt, w, CH), BF16)
    nshape = lambda w: jax.ShapeDtypeStruct((nb, t, w), BF16)
    return pl.pallas_call(
        _inproj_kernel,
        grid=(nb, nt),
        in_specs=[pl.BlockSpec((1, CH, d), lambda b, j: (b, j, 0)),
                  pl.BlockSpec((None, 6, d), _mod_index(nl, nb)),
                  _const_spec((1, d)),
                  _const_spec(wt.shape),
                  _const_spec(wn.shape),
                  pl.BlockSpec((1, 2, HEAD_DIM, CH), lambda b, j: (j, 0, 0, 0)),
                  pl.BlockSpec((3, CH, KV_PAIR), lambda b, j: (0, j, 0)),
                  _const_spec(gq_t.shape),
                  _const_spec(gk_n.shape)],
        out_specs=[tok(A_W), tok(A_W), tok(B_W), tok(B_KW), tok(C_W), tok(C_KW),
                   nat(A_W), nat(B_KW), nat(C_KW)],
        out_shape=[tshape(A_W), tshape(A_W), tshape(B_W), tshape(B_KW), tshape(C_W), tshape(C_KW),
                   nshape(A_W), nshape(B_KW), nshape(C_KW)],
        compiler_params=_cparams("parallel", "parallel"),
        name="in_projection",
    )(z, mod, g_pre, wt, wn, rope_t, rope_n, gq_t, gk_n)


def _with_ones(v_t):
    return jnp.concatenate([v_t, jnp.ones((DEN_ROWS, v_t.shape[1]), v_t.dtype)], axis=0)


def _softmax_pv(s, v_t, pv_fn, m_ref, acc_ref):
    m_prev = m_ref[...]
    m_new = jnp.maximum(m_prev, jnp.max(s, axis=0, keepdims=True))
    alpha = jnp.exp2(m_prev - m_new)
    p = jnp.exp2(s - m_new).astype(BF16)
    acc_ref[...] = alpha * acc_ref[...] + pv_fn(v_t, p)
    m_ref[...] = m_new


def _normalized(acc_ref):
    e = acc_ref.shape[0] - DEN_ROWS
    return acc_ref[:e, :] / acc_ref[e:e + 1, :]


def _cat_queries(q, heads, slot):
    parts = []
    for h in range(heads):
        qh = q[h * HEAD_DIM:(h + 1) * HEAD_DIM]
        z = jnp.zeros_like(qh)
        parts.append(jnp.concatenate([qh, z] if slot == 0 else [z, qh], axis=0))
    return jnp.concatenate(parts, axis=1)


def _heads_to_rows(o, heads):
    return jnp.concatenate([o[:, h * CH:(h + 1) * CH] for h in range(heads)], axis=0)


def _sweep_keys(is_latent, nl, k_ref, v_ref, score_fn, pv_fn, m_ref, acc_ref, s_ref):
    m_ref[...] = jnp.full(m_ref.shape, NEG, F32)
    acc_ref[...] = jnp.zeros(acc_ref.shape, F32)
    nblk = nl // KB

    def scores(b):
        return score_fn(k_ref[0, pl.ds(pl.multiple_of(b * (KB * CH), KB * CH), KB * CH), :])

    def values(b):
        return jnp.concatenate([v_ref[0, KB * b + c] for c in range(KB)], axis=1)

    s_ref[0] = scores(0)
    _softmax_pv(score_fn(k_ref[0, nl * CH:(nl + 1) * CH, :]), v_ref[0, nl], pv_fn, m_ref, acc_ref)

    def body(i, carry):
        b = NBUF * i
        for u in range(NBUF):
            s_ref[(u + 1) % NBUF] = scores(jnp.minimum(b + u + 1, nblk - 1))
            _softmax_pv(s_ref[u], values(b + u), pv_fn, m_ref, acc_ref)
        return carry

    lax.fori_loop(0, jnp.where(is_latent, nblk // NBUF, 0), body, 0)


def _grouped_fns(q_cats, k_lanes, v_rows):
    widths = [qc.shape[1] for qc in q_cats]
    starts = [sum(widths[:g]) for g in range(len(widths))]

    def score_fn(k):
        return jnp.concatenate(
            [jnp.dot(k[:, k_lanes[g]], q_cats[g], preferred_element_type=F32)
             for g in range(len(q_cats))], axis=1)

    def pv_fn(v_t, p):
        return jnp.concatenate(
            [jnp.dot(_with_ones(v_t[v_rows[g]]), p[:, starts[g]:starts[g] + widths[g]],
                     preferred_element_type=F32) for g in range(len(q_cats))], axis=1)

    return score_fn, pv_fn


def _attn_a_kernel(nl, lam_init, q_ref, k_ref, v_ref, lam_ref, gs_ref, o_ref,
                   m_ref, acc_ref, s_ref):
    ew = 2 * HEAD_DIM
    q = q_ref[0, 0]
    q_cats = [jnp.concatenate([_cat_queries(q[h * ew:h * ew + HEAD_DIM], 1, 0),
                               _cat_queries(q[h * ew + HEAD_DIM:(h + 1) * ew], 1, 1)], axis=1)
              for h in range(A_STEP)]
    lanes = [slice(h * ew, (h + 1) * ew) for h in range(A_STEP)]
    score_fn, pv_fn = _grouped_fns(q_cats, lanes, lanes)
    _sweep_keys(pl.program_id(2) < nl, nl, k_ref, v_ref, score_fn, pv_fn, m_ref, acc_ref, s_ref)

    lp = lam_ref[...]
    lam = (jnp.exp(jnp.sum(lp[0:1] * lp[1:2], axis=-1, keepdims=True))
           - jnp.exp(jnp.sum(lp[2:3] * lp[3:4], axis=-1, keepdims=True)) + lam_init)
    o = _normalized(acc_ref)
    outs = []
    for h in range(A_STEP):
        oh = o[:, 2 * h * CH:(2 * h + 1) * CH] - lam * o[:, (2 * h + 1) * CH:(2 * h + 2) * CH]
        oh = oh * lax.rsqrt(jnp.mean(oh * oh, axis=0, keepdims=True) + SUBLN_EPS) * gs_ref[...]
        outs.append(oh * (1.0 - lam_init))
    o_ref[0] = jnp.concatenate(outs, axis=0).T.astype(BF16)


def _attn_a_call(q_t, k, v_t, lam_p, gs_t, nl, nq, lam_init):
    nb, nt = q_t.shape[:2]
    t = k.shape[1]
    sw = A_STEP * 2 * HEAD_DIM
    qw = A_STEP * 2 * CH
    return pl.pallas_call(
        functools.partial(_attn_a_kernel, nl, lam_init),
        grid=(nb, A_HEADS // A_STEP, nq),
        in_specs=[pl.BlockSpec((1, 1, sw, CH), lambda b, h, i: (b, i, h, 0)),
                  pl.BlockSpec((1, t, sw), lambda b, h, i: (b, 0, h)),
                  pl.BlockSpec((1, nt, sw, CH), lambda b, h, i: (b, 0, h, 0)),
                  pl.BlockSpec(lam_p.shape, lambda b, h, i: (0, 0)),
                  pl.BlockSpec(gs_t.shape, lambda b, h, i: (0, 0))],
        out_specs=pl.BlockSpec((1, CH, sw), lambda b, h, i: (b, i, h)),
        out_shape=jax.ShapeDtypeStruct((nb, nq * CH, A_W), BF16),
        scratch_shapes=[pltpu.VMEM((1, qw), F32), pltpu.VMEM((2 * HEAD_DIM + DEN_ROWS, qw), F32),
                        pltpu.VMEM((NBUF, KB * CH, qw), F32)],
        compiler_params=_cparams("parallel", "parallel", "arbitrary"),
        name="attn_differential",
    )(q_t, k, v_t, lam_p, gs_t)


def _gqa_fns(q, heads, kv):
    group = heads // kv
    gw = group * HEAD_DIM
    q_cat = jnp.concatenate([_cat_queries(q[n * gw:(n + 1) * gw], group, n) for n in range(kv)],
                            axis=1)

    def score_fn(k):
        return jnp.dot(k, q_cat, preferred_element_type=F32)

    def pv_fn(v_t, p):
        return jnp.concatenate(
            [jnp.dot(_with_ones(v_t[n * HEAD_DIM:(n + 1) * HEAD_DIM]),
                     p[:, n * group * CH:(n + 1) * group * CH], preferred_element_type=F32)
             for n in range(kv)], axis=1)

    return score_fn, pv_fn


def _attn_b_kernel(nl, q_ref, k_ref, v_ref, o_ref, m_ref, acc_ref, s_ref):
    n = pl.program_id(1)
    group = B_HEADS // B_KV
    q = q_ref[0, 0]
    q_cat = jnp.where(n == 0, _cat_queries(q, group, 0), _cat_queries(q, group, 1))

    def score_fn(k):
        return jnp.dot(k, q_cat, preferred_element_type=F32)

    def pv_fn(v_t, p):
        return jnp.dot(_with_ones(v_t), p, preferred_element_type=F32)

    _sweep_keys(pl.program_id(2) < nl, nl, k_ref, v_ref, score_fn, pv_fn, m_ref, acc_ref, s_ref)
    o_ref[0] = _heads_to_rows(_normalized(acc_ref), group).T.astype(BF16)


def _attn_b_call(q_t, k, v_t, nl, nq):
    nb, nt = q_t.shape[:2]
    t = k.shape[1]
    group = B_HEADS // B_KV
    gw = group * HEAD_DIM
    qw = group * CH
    return pl.pallas_call(
        functools.partial(_attn_b_kernel, nl),
        grid=(nb, B_KV, nq),
        in_specs=[pl.BlockSpec((1, 1, gw, CH), lambda b, n, i: (b, i, n, 0)),
                  pl.BlockSpec((1, t, KV_PAIR), lambda b, n, i: (b, 0, 0)),
                  pl.BlockSpec((1, nt, HEAD_DIM, CH), lambda b, n, i: (b, 0, n, 0))],
        out_specs=pl.BlockSpec((1, CH, gw), lambda b, n, i: (b, i, n)),
        out_shape=jax.ShapeDtypeStruct((nb, nq * CH, B_W), BF16),
        scratch_shapes=[pltpu.VMEM((1, qw), F32), pltpu.VMEM((HEAD_DIM + DEN_ROWS, qw), F32),
                        pltpu.VMEM((NBUF, KB * CH, qw), F32)],
        compiler_params=_cparams("parallel", "parallel", "arbitrary"),
        name="attn_dense_gqa",
    )(q_t, k, v_t)


def _attn_c_kernel(nl, sink_ref, q_ref, bias_ref, k0_ref, k1_ref, k2_ref, kx_ref,
                   v0_ref, v1_ref, v2_ref, vx_ref, o_ref, m_ref, acc_ref):
    qi = pl.program_id(1)
    hw = CH - WINDOW
    score_fn, pv_fn = _gqa_fns(q_ref[0, 0], C_HEADS, C_KV)
    k = jnp.concatenate([k0_ref[0, hw:, :], k1_ref[0], k2_ref[0, :WINDOW, :], kx_ref[0]], axis=0)
    v_t = jnp.concatenate([v0_ref[0, 0, :, hw:], v1_ref[0, 0], v2_ref[0, 0, :, :WINDOW], vx_ref[0, 0]],
                          axis=1)
    is_latent = qi < nl
    bias = bias_ref[...]
    bias = jnp.concatenate([
        jnp.where(is_latent & (qi >= 1), bias[:WINDOW], NEG),
        jnp.where(is_latent, bias[WINDOW:WINDOW + CH], NEG),
        jnp.where(is_latent & (qi + 1 < nl), bias[WINDOW + CH:], NEG),
        jnp.zeros((CH, CH), F32)], axis=0)
    s = score_fn(k)
    s = jnp.concatenate([s[:, h * CH:(h + 1) * CH] + bias for h in range(C_HEADS)], axis=1)
    m_ref[...] = jnp.concatenate(
        [jnp.full((1, CH), sink_ref[h] * LOG2E, F32) for h in range(C_HEADS)], axis=1)
    acc_ref[:HEAD_DIM, :] = jnp.zeros((HEAD_DIM, C_HEADS * CH), F32)
    acc_ref[HEAD_DIM:, :] = jnp.ones((DEN_ROWS, C_HEADS * CH), F32)
    _softmax_pv(s, v_t, pv_fn, m_ref, acc_ref)
    o_ref[0] = _heads_to_rows(_normalized(acc_ref), C_HEADS).T.astype(BF16)


def _window_bias():
    key = jnp.arange(-WINDOW, CH + WINDOW, dtype=jnp.int32)[:, None]
    qry = jnp.arange(CH, dtype=jnp.int32)[None, :]
    return jnp.where(jnp.abs(key - qry) <= WINDOW, 0.0, NEG).astype(F32)


def _attn_c_call(sink, q_t, k, v_t, nl, nq):
    nb, nt = q_t.shape[:2]
    bias = _window_bias()
    band = lambda c: (lambda b, i: (b, jnp.clip(i + (c - 1), 0, nl - 1), 0))
    band_t = lambda c: (lambda b, i: (b, jnp.clip(i + (c - 1), 0, nl - 1), 0, 0))
    k_spec = lambda f: pl.BlockSpec((1, CH, KV_PAIR), f)
    v_spec = lambda f: pl.BlockSpec((1, 1, C_KW, CH), f)
    qw = C_HEADS * CH
    return pl.pallas_call(
        functools.partial(_attn_c_kernel, nl),
        grid=(nb, nq),
        in_specs=[pl.BlockSpec(memory_space=pltpu.SMEM),
                  pl.BlockSpec((1, 1, C_W, CH), lambda b, i: (b, i, 0, 0)),
                  _const_spec(bias.shape),
                  k_spec(band(0)), k_spec(band(1)), k_spec(band(2)),
                  k_spec(lambda b, i: (b, nl, 0)),
                  v_spec(band_t(0)), v_spec(band_t(1)), v_spec(band_t(2)),
                  v_spec(lambda b, i: (b, nl, 0, 0))],
        out_specs=pl.BlockSpec((1, CH, C_W), lambda b, i: (b, i, 0)),
        out_shape=jax.ShapeDtypeStruct((nb, nq * CH, C_W), BF16),
        scratch_shapes=[pltpu.VMEM((1, qw), F32), pltpu.VMEM((HEAD_DIM + DEN_ROWS, qw), F32)],
        compiler_params=_cparams("parallel", "parallel"),
        name="attn_window",
    )(sink, q_t, bias, k, k, k, k, v_t, v_t, v_t, v_t)


def _merge_kernel(z_ref, mod_ref, gpre_ref, gpost_ref, oa_ref, ob_ref, oc_ref,
                  wg_ref, wbr_ref, wo_ref, out_ref):
    x = z_ref[0]
    m = mod_ref[...]
    d = x.shape[-1]
    xb = _modulate(x, gpre_ref[...], m[0:1], m[1:2]).astype(BF16)
    mixed = None
    row = 0
    for i, o_ref in enumerate((oa_ref, ob_ref, oc_ref)):
        w = o_ref.shape[-1]
        gate = jax.nn.sigmoid(jnp.dot(xb, wg_ref[:, i * d:(i + 1) * d], preferred_element_type=F32))
        term = gate * jnp.dot(o_ref[0], wbr_ref[row:row + w, :], preferred_element_type=F32)
        mixed = term if mixed is None else mixed + term
        row += w
    mix = jnp.dot(mixed.astype(BF16), wo_ref[...], preferred_element_type=F32)
    out_ref[0] = x + m[2:3] * _rms(mix, gpost_ref[...], EPS)


def _merge_call(z, mod, g_pre, g_post, oa, ob, oc, wg, wbr, wo, nl, nq):
    nb, _, d = z.shape
    tok = lambda w: pl.BlockSpec((1, CH, w), lambda b, j: (b, j, 0))
    return pl.pallas_call(
        _merge_kernel,
        grid=(nb, nq),
        in_specs=[tok(d), pl.BlockSpec((None, 6, d), _mod_index(nl, nb)),
                  _const_spec((1, d)), _const_spec((1, d)),
                  tok(A_W), tok(B_W), tok(C_W),
                  _const_spec(wg.shape), _const_spec(wbr.shape), _const_spec(wo.shape)],
        out_specs=tok(d),
        out_shape=jax.ShapeDtypeStruct((nb, nq * CH, d), F32),
        compiler_params=_cparams("parallel", "parallel"),
        name="merge_branches",
    )(z, mod, g_pre, g_post, oa, ob, oc, wg, wbr, wo)


def _mlp_kernel(z_ref, mod_ref, gpre_ref, gpost_ref, w1_ref, w2_ref, out_ref):
    x = z_ref[0]
    m = mod_ref[...]
    xb = _modulate(x, gpre_ref[...], m[3:4], m[4:5]).astype(BF16)
    h = jnp.maximum(jnp.dot(xb, w1_ref[...], preferred_element_type=F32), 0.0)
    y = jnp.dot((h * h).astype(BF16), w2_ref[...], preferred_element_type=F32)
    out_ref[0] = x + m[5:6] * _rms(y, gpost_ref[...], EPS)


def _mlp_call(z, mod, g_pre, g_post, w1, w2, nl):
    nb, t, d = z.shape
    tok = pl.BlockSpec((1, CH, d), lambda b, j: (b, j, 0))
    return pl.pallas_call(
        _mlp_kernel,
        grid=(nb, t // CH),
        in_specs=[tok, pl.BlockSpec((None, 6, d), _mod_index(nl, nb)),
                  _const_spec((1, d)), _const_spec((1, d)),
                  _const_spec(w1.shape), _const_spec(w2.shape)],
        out_specs=tok,
        out_shape=jax.ShapeDtypeStruct((nb, t, d), F32),
        compiler_params=_cparams("parallel", "parallel"),
        name="sq_relu_mlp",
    )(z, mod, g_pre, g_post, w1, w2)


def _rope_tables(seq, ctx_len):
    pos = jnp.arange(seq, dtype=jnp.int32)
    row = (pos // GRID_W).astype(F32)
    col = (pos % GRID_W).astype(F32)
    half = HEAD_DIM // 2
    inv_freq = jnp.power(ROPE_THETA, -jnp.arange(0, half, 2, dtype=F32) / half)
    ang_r = row[:, None] * inv_freq[None, :]
    ang_c = col[:, None] * inv_freq[None, :]
    cos = jnp.concatenate([jnp.cos(ang_r)] * 2 + [jnp.cos(ang_c)] * 2, axis=-1)
    sin = jnp.concatenate([-jnp.sin(ang_r), jnp.sin(ang_r), -jnp.sin(ang_c), jnp.sin(ang_c)], axis=-1)
    cos = jnp.concatenate([cos, jnp.ones((ctx_len, HEAD_DIM), F32)], axis=0)
    sin = jnp.concatenate([sin, jnp.zeros((ctx_len, HEAD_DIM), F32)], axis=0)
    nt = (seq + ctx_len) // CH
    to_t = lambda a: a.reshape(nt, CH, HEAD_DIM).transpose(0, 2, 1)
    rope_t = jnp.stack([to_t(cos), to_t(sin)], axis=1)
    lower = (jnp.arange(HEAD_DIM) % (half)) < (half // 2)
    pair = lambda a: jnp.concatenate([a, a], axis=-1)
    rope_n = jnp.stack([pair(cos), pair(jnp.where(lower, sin, 0.0)),
                        pair(jnp.where(lower, 0.0, sin))], axis=0)
    return rope_t, rope_n


def kernel(x, c, ctx, c_ctx, w_ada, b_ada, g_pre_mix, g_post_mix, g_pre_ff, g_post_ff, w_in, g_qnorm,
           g_knorm, lam_q1, lam_k1, lam_q2, lam_k2, g_subln, sink, w_branch, w_out, w_ff1, w_ff2):
    nb, seq, d = x.shape
    ctx_len = ctx.shape[1]
    depth = w_ada.shape[0]
    assert seq % (NBUF * KB * CH) == 0 and ctx_len == CH and nb + 1 <= MOD_ROWS and seq % GRID_W == 0
    nl = seq // CH
    nt = nl + 1

    z = jnp.concatenate([x, ctx], axis=1)
    cvec = jnp.zeros((MOD_ROWS, d), F32).at[:nb].set(c).at[nb].set(c_ctx)
    mods = _mod_call(cvec, w_ada, b_ada).reshape(depth, MOD_ROWS, 6, d)
    rope_t, rope_n = _rope_tables(seq, ctx_len)

    splits = (A_W, A_W, A_W, B_W, B_KW, B_KW, C_W, C_KW, C_KW)
    offs = [0]
    for s in splits:
        offs.append(offs[-1] + s)
    col = lambda w, i: w[:, offs[i]:offs[i + 1]]

    for l in range(depth):
        last = l == depth - 1
        nq = nl if last else nt
        lam_init = 0.8 - 0.6 * math.exp(-0.3 * l)
        w = w_in[l]
        wt = jnp.concatenate([col(w, i) for i in (0, 2, 3, 5, 6, 8)], axis=1).T.astype(BF16)
        wn = jnp.concatenate([col(w, i) for i in (1, 4, 7)], axis=1).astype(BF16)
        wg = w[:, offs[-1]:].astype(BF16)
        row = lambda g: g[l].reshape(1, -1)
        gq_t = jnp.broadcast_to(g_qnorm[l][:, None], (HEAD_DIM, CH))
        gk_n = jnp.concatenate([g_knorm[l]] * B_KV).reshape(1, KV_PAIR)
        gs_t = jnp.broadcast_to(g_subln[l][:, None], (2 * HEAD_DIM, CH))
        lam_p = jnp.stack([lam_q1[l], lam_k1[l], lam_q2[l], lam_k2[l]], axis=0)

        qa, va, qb, vb, qc, vc, ka, kb, kc = _inproj_call(
            z, mods[l], row(g_pre_mix), wt, wn, rope_t, rope_n, gq_t, gk_n, nl)
        oa = _attn_a_call(qa, ka, va, lam_p, gs_t, nl, nq, lam_init)
        ob = _attn_b_call(qb, kb, vb, nl, nq)
        oc = _attn_c_call(sink[l], qc, kc, vc, nl, nq)
        z = _merge_call(z, mods[l], row(g_pre_mix), row(g_post_mix), oa, ob, oc, wg,
                        w_branch[l].astype(BF16), w_out[l].astype(BF16), nl, nq)
        z = _mlp_call(z, mods[l], row(g_pre_ff), row(g_post_ff),
                      w_ff1[l].astype(BF16), w_ff2[l].astype(BF16), nl)
    return z
```

```python
import functools
import math

import jax
import jax.numpy as jnp
from jax import lax
from jax.experimental import pallas as pl
from jax.experimental.pallas import tpu as pltpu

HEAD_DIM = 64
A_HEADS = 4
B_HEADS = 8
B_KV = 2
C_HEADS = 8
C_KV = 2
WINDOW = 128
GRID_W = 64
ROPE_THETA = 10000.0
EPS = 1e-6
SUBLN_EPS = 1e-5

A_W = A_HEADS * 2 * HEAD_DIM
B_W = B_HEADS * HEAD_DIM
C_W = C_HEADS * HEAD_DIM
B_KW = B_KV * HEAD_DIM
C_KW = C_KV * HEAD_DIM
KV_PAIR = 2 * HEAD_DIM

CH = 256
KB = 2
NBUF = 4
A_STEP = 2
MOD_ROWS = 8
NEG = -1e30
LOG2E = math.log2(math.e)
DEN_ROWS = 16
VMEM_LIMIT = 48 * 1024 * 1024

F32 = jnp.float32
BF16 = jnp.bfloat16


def _cparams(*sem):
    return pltpu.CompilerParams(dimension_semantics=sem, vmem_limit_bytes=VMEM_LIMIT)


def _const_spec(shape):
    nd = len(shape)
    return pl.BlockSpec(shape, lambda *_: (0,) * nd, pipeline_mode=pl.Buffered(1))


def _rms(x, gain, eps):
    return x * lax.rsqrt(jnp.mean(x * x, axis=-1, keepdims=True) + eps) * gain


def _modulate(x, gain, shift, scale):
    return _rms(x, gain, EPS) * (1.0 + scale) + shift


def _mod_kernel(c_ref, w_ref, b_ref, o_ref):
    c = c_ref[...]
    s = c * jax.nn.sigmoid(c)
    o_ref[0] = jnp.dot(s, w_ref[0], preferred_element_type=F32,
                       precision=lax.Precision.HIGHEST) + b_ref[0]


def _mod_call(cvec, w_ada, b_ada):
    depth, d, n = w_ada.shape
    tn = n // 4
    return pl.pallas_call(
        _mod_kernel,
        grid=(depth, n // tn),
        in_specs=[pl.BlockSpec((MOD_ROWS, d), lambda l, j: (0, 0)),
                  pl.BlockSpec((1, d, tn), lambda l, j: (l, 0, j)),
                  pl.BlockSpec((1, 1, tn), lambda l, j: (l, 0, j))],
        out_specs=pl.BlockSpec((1, MOD_ROWS, tn), lambda l, j: (l, 0, j)),
        out_shape=jax.ShapeDtypeStruct((depth, MOD_ROWS, n), F32),
        compiler_params=_cparams("parallel", "parallel"),
        name="adaln_vectors",
    )(cvec, w_ada, b_ada.reshape(depth, 1, n))


def _rope_t(y3, cos_t, sin_t):
    part = jnp.concatenate([y3[:, 16:32], y3[:, 0:16], y3[:, 48:64], y3[:, 32:48]], axis=1)
    return y3 * cos_t[None] + part * sin_t[None]


def _rope_n(y, cos_n, sin_lo, sin_hi):
    return (y * cos_n + pltpu.roll(y, KV_PAIR - 16, axis=1) * sin_lo
            + pltpu.roll(y, 16, axis=1) * sin_hi)


def _inproj_kernel(z_ref, mod_ref, g_ref, wt_ref, wn_ref, rt_ref, rn_ref, gq_ref, gk_ref,
                   qa_ref, va_ref, qb_ref, vb_ref, qc_ref, vc_ref, ka_ref, kb_ref, kc_ref):
    m = mod_ref[...]
    xb = _modulate(z_ref[0], g_ref[...], m[0:1], m[1:2]).astype(BF16)
    cos_t, sin_t = rt_ref[0, 0], rt_ref[0, 1]
    cos_n, sin_lo, sin_hi = rn_ref[0], rn_ref[1], rn_ref[2]
    ch = xb.shape[0]
    q_scale = HEAD_DIM ** -0.5 * LOG2E

    def proj_t(lo, hi):
        return lax.dot_general(wt_ref[lo:hi, :], xb, (((1,), (1,)), ((), ())),
                               preferred_element_type=F32)

    def store_q(ref, y3):
        ref[0, 0] = (_rope_t(y3, cos_t, sin_t) * q_scale).reshape(-1, ch).astype(BF16)

    o = 0
    store_q(qa_ref, proj_t(o, o + A_W).reshape(-1, HEAD_DIM, ch))
    o += A_W
    va_ref[0, 0] = proj_t(o, o + A_W).astype(BF16)
    o += A_W
    qb = proj_t(o, o + B_W).reshape(-1, HEAD_DIM, ch)
    qb = qb * lax.rsqrt(jnp.mean(qb * qb, axis=1, keepdims=True) + EPS) * gq_ref[...][None]
    store_q(qb_ref, qb)
    o += B_W
    vb_ref[0, 0] = proj_t(o, o + B_KW).astype(BF16)
    o += B_KW
    store_q(qc_ref, proj_t(o, o + C_W).reshape(-1, HEAD_DIM, ch))
    o += C_W
    vc_ref[0, 0] = proj_t(o, o + C_KW).astype(BF16)

    kn = jnp.dot(xb, wn_ref[...], preferred_element_type=F32)
    for i in range(A_W // KV_PAIR):
        ka_ref[0, :, i * KV_PAIR:(i + 1) * KV_PAIR] = _rope_n(
            kn[:, i * KV_PAIR:(i + 1) * KV_PAIR], cos_n, sin_lo, sin_hi).astype(BF16)
    kb = kn[:, A_W:A_W + B_KW]
    sq = kb * kb
    first = lax.broadcasted_iota(jnp.int32, sq.shape, 1) < HEAD_DIM
    s0 = jnp.sum(jnp.where(first, sq, 0.0), axis=-1, keepdims=True)
    s1 = jnp.sum(jnp.where(first, 0.0, sq), axis=-1, keepdims=True)
    ms = jnp.where(first, s0, s1) * (1.0 / HEAD_DIM)
    kb = kb * lax.rsqrt(ms + EPS) * gk_ref[...]
    kb_ref[0] = _rope_n(kb, cos_n, sin_lo, sin_hi).astype(BF16)
    kc_ref[0] = _rope_n(kn[:, A_W + B_KW:], cos_n, sin_lo, sin_hi).astype(BF16)


def _mod_index(nl, nb):
    return lambda b, j: (jnp.where(j < nl, b, nb), 0, 0)


def _inproj_call(z, mod, g_pre, wt, wn, rope_t, rope_n, gq_t, gk_n, nl):
    nb, t, d = z.shape
    nt = t // CH
    tok = lambda w: pl.BlockSpec((1, 1, w, CH), lambda b, j: (b, j, 0, 0))
    nat = lambda w: pl.BlockSpec((1, CH, w), lambda b, j: (b, j, 0))
    tshape = lambda w: jax.ShapeDtypeStruct((nb, nt, w, CH), BF16)
    nshape = lambda w: jax.ShapeDtypeStruct((nb, t, w), BF16)
    return pl.pallas_call(
        _inproj_kernel,
        grid=(nb, nt),
        in_specs=[pl.BlockSpec((1, CH, d), lambda b, j: (b, j, 0)),
                  pl.BlockSpec((None, 6, d), _mod_index(nl, nb)),
                  _const_spec((1, d)),
                  _const_spec(wt.shape),
                  _const_spec(wn.shape),
                  pl.BlockSpec((1, 2, HEAD_DIM, CH), lambda b, j: (j, 0, 0, 0)),
                  pl.BlockSpec((3, CH, KV_PAIR), lambda b, j: (0, j, 0)),
                  _const_spec(gq_t.shape),
                  _const_spec(gk_n.shape)],
        out_specs=[tok(A_W), tok(A_W), tok(B_W), tok(B_KW), tok(C_W), tok(C_KW),
                   nat(A_W), nat(B_KW), nat(C_KW)],
        out_shape=[tshape(A_W), tshape(A_W), tshape(B_W), tshape(B_KW), tshape(C_W), tshape(C_KW),
                   nshape(A_W), nshape(B_KW), nshape(C_KW)],
        compiler_params=_cparams("parallel", "parallel"),
        name="in_projection",
    )(z, mod, g_pre, wt, wn, rope_t, rope_n, gq_t, gk_n)


def _with_ones(v_t):
    return jnp.concatenate([v_t, jnp.ones((DEN_ROWS, v_t.shape[1]), v_t.dtype)], axis=0)


def _softmax_pv(s, s_max, v_t, pv_fn, m_ref, acc_ref):
    m_prev = m_ref[...]
    m_new = jnp.maximum(m_prev, s_max)
    alpha = jnp.exp2(m_prev - m_new)
    p = jnp.exp2(s - m_new).astype(BF16)
    acc_ref[...] = alpha * acc_ref[...] + pv_fn(v_t, p)
    m_ref[...] = m_new


def _normalized(acc_ref):
    e = acc_ref.shape[0] - DEN_ROWS
    return acc_ref[:e, :] / acc_ref[e:e + 1, :]


def _cat_queries(q, heads, slot):
    parts = []
    for h in range(heads):
        qh = q[h * HEAD_DIM:(h + 1) * HEAD_DIM]
        z = jnp.zeros_like(qh)
        parts.append(jnp.concatenate([qh, z] if slot == 0 else [z, qh], axis=0))
    return jnp.concatenate(parts, axis=1)


def _heads_to_rows(o, heads):
    return jnp.concatenate([o[:, h * CH:(h + 1) * CH] for h in range(heads)], axis=0)


def _sweep_keys(is_latent, nl, k_ref, v_ref, score_fn, pv_fn, m_ref, acc_ref, s_ref, smax_ref):
    m_ref[...] = jnp.full(m_ref.shape, NEG, F32)
    acc_ref[...] = jnp.zeros(acc_ref.shape, F32)
    nblk = nl // KB

    def put_scores(u, lo, size):
        s = score_fn(k_ref[0, lo:lo + size, :])
        s_ref[u, :size] = s
        smax_ref[u] = jnp.max(s, axis=0, keepdims=True)

    def fold(u, size, v_t):
        _softmax_pv(s_ref[u, :size], smax_ref[u], v_t, pv_fn, m_ref, acc_ref)

    @pl.when(is_latent)
    def _():
        put_scores(0, 0, KB * CH)
        for b in range(nblk):
            if b + 1 < nblk:
                put_scores((b + 1) % NBUF, (b + 1) * KB * CH, KB * CH)
            else:
                put_scores((b + 1) % NBUF, nl * CH, CH)
            fold(b % NBUF, KB * CH,
                 jnp.concatenate([v_ref[0, KB * b + c] for c in range(KB)], axis=1))
        fold(nblk % NBUF, CH, v_ref[0, nl])

    @pl.when(jnp.logical_not(is_latent))
    def _():
        put_scores(0, nl * CH, CH)
        fold(0, CH, v_ref[0, nl])


def _grouped_fns(q_cats, k_lanes, v_rows):
    widths = [qc.shape[1] for qc in q_cats]
    starts = [sum(widths[:g]) for g in range(len(widths))]

    def score_fn(k):
        return jnp.concatenate(
            [jnp.dot(k[:, k_lanes[g]], q_cats[g], preferred_element_type=F32)
             for g in range(len(q_cats))], axis=1)

    def pv_fn(v_t, p):
        return jnp.concatenate(
            [jnp.dot(_with_ones(v_t[v_rows[g]]), p[:, starts[g]:starts[g] + widths[g]],
                     preferred_element_type=F32) for g in range(len(q_cats))], axis=1)

    return score_fn, pv_fn


def _attn_a_kernel(nl, lam_init, q_ref, k_ref, v_ref, lam_ref, gs_ref, o_ref,
                   m_ref, acc_ref, s_ref, smax_ref):
    ew = 2 * HEAD_DIM
    q = q_ref[0, 0]
    q_cats = [jnp.concatenate([_cat_queries(q[h * ew:h * ew + HEAD_DIM], 1, 0),
                               _cat_queries(q[h * ew + HEAD_DIM:(h + 1) * ew], 1, 1)], axis=1)
              for h in range(A_STEP)]
    lanes = [slice(h * ew, (h + 1) * ew) for h in range(A_STEP)]
    score_fn, pv_fn = _grouped_fns(q_cats, lanes, lanes)
    _sweep_keys(pl.program_id(2) < nl, nl, k_ref, v_ref, score_fn, pv_fn, m_ref, acc_ref, s_ref,
                smax_ref)

    lp = lam_ref[...]
    lam = (jnp.exp(jnp.sum(lp[0:1] * lp[1:2], axis=-1, keepdims=True))
           - jnp.exp(jnp.sum(lp[2:3] * lp[3:4], axis=-1, keepdims=True)) + lam_init)
    o = _normalized(acc_ref)
    outs = []
    for h in range(A_STEP):
        oh = o[:, 2 * h * CH:(2 * h + 1) * CH] - lam * o[:, (2 * h + 1) * CH:(2 * h + 2) * CH]
        oh = oh * lax.rsqrt(jnp.mean(oh * oh, axis=0, keepdims=True) + SUBLN_EPS) * gs_ref[...]
        outs.append(oh * (1.0 - lam_init))
    o_ref[0] = jnp.concatenate(outs, axis=0).T.astype(BF16)


def _attn_a_call(q_t, k, v_t, lam_p, gs_t, nl, nq, lam_init):
    nb, nt = q_t.shape[:2]
    t = k.shape[1]
    sw = A_STEP * 2 * HEAD_DIM
    qw = A_STEP * 2 * CH
    return pl.pallas_call(
        functools.partial(_attn_a_kernel, nl, lam_init),
        grid=(nb, A_HEADS // A_STEP, nq),
        in_specs=[pl.BlockSpec((1, 1, sw, CH), lambda b, h, i: (b, i, h, 0)),
                  pl.BlockSpec((1, t, sw), lambda b, h, i: (b, 0, h)),
                  pl.BlockSpec((1, nt, sw, CH), lambda b, h, i: (b, 0, h, 0)),
                  pl.BlockSpec(lam_p.shape, lambda b, h, i: (0, 0)),
                  pl.BlockSpec(gs_t.shape, lambda b, h, i: (0, 0))],
        out_specs=pl.BlockSpec((1, CH, sw), lambda b, h, i: (b, i, h)),
        out_shape=jax.ShapeDtypeStruct((nb, nq * CH, A_W), BF16),
        scratch_shapes=[pltpu.VMEM((1, qw), F32), pltpu.VMEM((2 * HEAD_DIM + DEN_ROWS, qw), F32),
                        pltpu.VMEM((NBUF, KB * CH, qw), F32), pltpu.VMEM((NBUF, 1, qw), F32)],
        compiler_params=_cparams("parallel", "parallel", "arbitrary"),
        name="attn_differential",
    )(q_t, k, v_t, lam_p, gs_t)


def _attn_b_kernel(nl, q_ref, k_ref, v_ref, o_ref, m_ref, acc_ref, s_ref, smax_ref):
    n = pl.program_id(1)
    group = B_HEADS // B_KV
    q = q_ref[0, 0]
    q_cat = jnp.where(n == 0, _cat_queries(q, group, 0), _cat_queries(q, group, 1))

    def score_fn(k):
        return jnp.dot(k, q_cat, preferred_element_type=F32)

    def pv_fn(v_t, p):
        return jnp.dot(_with_ones(v_t), p, preferred_element_type=F32)

    _sweep_keys(pl.program_id(2) < nl, nl, k_ref, v_ref, score_fn, pv_fn, m_ref, acc_ref, s_ref,
                smax_ref)
    o_ref[0] = _heads_to_rows(_normalized(acc_ref), group).T.astype(BF16)


def _attn_b_call(q_t, k, v_t, nl, nq):
    nb, nt = q_t.shape[:2]
    t = k.shape[1]
    group = B_HEADS // B_KV
    gw = group * HEAD_DIM
    qw = group * CH
    return pl.pallas_call(
        functools.partial(_attn_b_kernel, nl),
        grid=(nb, B_KV, nq),
        in_specs=[pl.BlockSpec((1, 1, gw, CH), lambda b, n, i: (b, i, n, 0)),
                  pl.BlockSpec((1, t, KV_PAIR), lambda b, n, i: (b, 0, 0)),
                  pl.BlockSpec((1, nt, HEAD_DIM, CH), lambda b, n, i: (b, 0, n, 0))],
        out_specs=pl.BlockSpec((1, CH, gw), lambda b, n, i: (b, i, n)),
        out_shape=jax.ShapeDtypeStruct((nb, nq * CH, B_W), BF16),
        scratch_shapes=[pltpu.VMEM((1, qw), F32), pltpu.VMEM((HEAD_DIM + DEN_ROWS, qw), F32),
                        pltpu.VMEM((NBUF, KB * CH, qw), F32), pltpu.VMEM((NBUF, 1, qw), F32)],
        compiler_params=_cparams("parallel", "parallel", "arbitrary"),
        name="attn_dense_gqa",
    )(q_t, k, v_t)


def _gqa_fns(q, heads, kv):
    group = heads // kv
    gw = group * HEAD_DIM
    q_cat = jnp.concatenate([_cat_queries(q[n * gw:(n + 1) * gw], group, n) for n in range(kv)],
                            axis=1)

    def score_fn(k):
        return jnp.dot(k, q_cat, preferred_element_type=F32)

    def pv_fn(v_t, p):
        return jnp.concatenate(
            [jnp.dot(_with_ones(v_t[n * HEAD_DIM:(n + 1) * HEAD_DIM]),
                     p[:, n * group * CH:(n + 1) * group * CH], preferred_element_type=F32)
             for n in range(kv)], axis=1)

    return score_fn, pv_fn


def _attn_c_kernel(nl, sink_ref, q_ref, bias_ref, k0_ref, k1_ref, k2_ref, kx_ref,
                   v0_ref, v1_ref, v2_ref, vx_ref, o_ref, m_ref, acc_ref):
    qi = pl.program_id(1)
    hw = CH - WINDOW
    score_fn, pv_fn = _gqa_fns(q_ref[0, 0], C_HEADS, C_KV)
    k = jnp.concatenate([k0_ref[0, hw:, :], k1_ref[0], k2_ref[0, :WINDOW, :], kx_ref[0]], axis=0)
    v_t = jnp.concatenate([v0_ref[0, 0, :, hw:], v1_ref[0, 0], v2_ref[0, 0, :, :WINDOW], vx_ref[0, 0]],
                          axis=1)
    is_latent = qi < nl
    bias = bias_ref[...]
    bias = jnp.concatenate([
        jnp.where(is_latent & (qi >= 1), bias[:WINDOW], NEG),
        jnp.where(is_latent, bias[WINDOW:WINDOW + CH], NEG),
        jnp.where(is_latent & (qi + 1 < nl), bias[WINDOW + CH:], NEG),
        jnp.zeros((CH, CH), F32)], axis=0)
    s = score_fn(k)
    s = jnp.concatenate([s[:, h * CH:(h + 1) * CH] + bias for h in range(C_HEADS)], axis=1)
    m_ref[...] = jnp.concatenate(
        [jnp.full((1, CH), sink_ref[h] * LOG2E, F32) for h in range(C_HEADS)], axis=1)
    acc_ref[:HEAD_DIM, :] = jnp.zeros((HEAD_DIM, C_HEADS * CH), F32)
    acc_ref[HEAD_DIM:, :] = jnp.ones((DEN_ROWS, C_HEADS * CH), F32)
    _softmax_pv(s, jnp.max(s, axis=0, keepdims=True), v_t, pv_fn, m_ref, acc_ref)
    o_ref[0] = _heads_to_rows(_normalized(acc_ref), C_HEADS).T.astype(BF16)


def _window_bias():
    key = jnp.arange(-WINDOW, CH + WINDOW, dtype=jnp.int32)[:, None]
    qry = jnp.arange(CH, dtype=jnp.int32)[None, :]
    return jnp.where(jnp.abs(key - qry) <= WINDOW, 0.0, NEG).astype(F32)


def _attn_c_call(sink, q_t, k, v_t, nl, nq):
    nb, nt = q_t.shape[:2]
    bias = _window_bias()
    band = lambda c: (lambda b, i: (b, jnp.clip(i + (c - 1), 0, nl - 1), 0))
    band_t = lambda c: (lambda b, i: (b, jnp.clip(i + (c - 1), 0, nl - 1), 0, 0))
    k_spec = lambda f: pl.BlockSpec((1, CH, KV_PAIR), f)
    v_spec = lambda f: pl.BlockSpec((1, 1, C_KW, CH), f)
    qw = C_HEADS * CH
    return pl.pallas_call(
        functools.partial(_attn_c_kernel, nl),
        grid=(nb, nq),
        in_specs=[pl.BlockSpec(memory_space=pltpu.SMEM),
                  pl.BlockSpec((1, 1, C_W, CH), lambda b, i: (b, i, 0, 0)),
                  _const_spec(bias.shape),
                  k_spec(band(0)), k_spec(band(1)), k_spec(band(2)),
                  k_spec(lambda b, i: (b, nl, 0)),
                  v_spec(band_t(0)), v_spec(band_t(1)), v_spec(band_t(2)),
                  v_spec(lambda b, i: (b, nl, 0, 0))],
        out_specs=pl.BlockSpec((1, CH, C_W), lambda b, i: (b, i, 0)),
        out_shape=jax.ShapeDtypeStruct((nb, nq * CH, C_W), BF16),
        scratch_shapes=[pltpu.VMEM((1, qw), F32), pltpu.VMEM((HEAD_DIM + DEN_ROWS, qw), F32)],
        compiler_params=_cparams("parallel", "parallel"),
        name="attn_window",
    )(sink, q_t, bias, k, k, k, k, v_t, v_t, v_t, v_t)


def _merge_mlp_kernel(z_ref, mod_ref, gmix_ref, gff_ref, oa_ref, ob_ref, oc_ref,
                      wg_ref, wbr_ref, wo_ref, w1_ref, w2_ref, out_ref):
    x = z_ref[0]
    m = mod_ref[...]
    d = x.shape[-1]
    xb = _modulate(x, gmix_ref[0:1], m[0:1], m[1:2]).astype(BF16)
    mixed = None
    row = 0
    for i, o_ref in enumerate((oa_ref, ob_ref, oc_ref)):
        w = o_ref.shape[-1]
        gate = jax.nn.sigmoid(jnp.dot(xb, wg_ref[:, i * d:(i + 1) * d], preferred_element_type=F32))
        term = gate * jnp.dot(o_ref[0], wbr_ref[row:row + w, :], preferred_element_type=F32)
        mixed = term if mixed is None else mixed + term
        row += w
    mix = jnp.dot(mixed.astype(BF16), wo_ref[...], preferred_element_type=F32)
    x = x + m[2:3] * _rms(mix, gmix_ref[1:2], EPS)

    xb = _modulate(x, gff_ref[0:1], m[3:4], m[4:5]).astype(BF16)
    h = jnp.maximum(jnp.dot(xb, w1_ref[...], preferred_element_type=F32), 0.0)
    y = jnp.dot((h * h).astype(BF16), w2_ref[...], preferred_element_type=F32)
    out_ref[0] = x + m[5:6] * _rms(y, gff_ref[1:2], EPS)


def _merge_mlp_call(z, mod, g_mix, g_ff, oa, ob, oc, wg, wbr, wo, w1, w2, nl, nq):
    nb, _, d = z.shape
    tok = lambda w: pl.BlockSpec((1, CH, w), lambda b, j: (b, j, 0))
    return pl.pallas_call(
        _merge_mlp_kernel,
        grid=(nb, nq),
        in_specs=[tok(d), pl.BlockSpec((None, 6, d), _mod_index(nl, nb)),
                  _const_spec(g_mix.shape), _const_spec(g_ff.shape),
                  tok(A_W), tok(B_W), tok(C_W),
                  _const_spec(wg.shape), _const_spec(wbr.shape), _const_spec(wo.shape),
                  _const_spec(w1.shape), _const_spec(w2.shape)],
        out_specs=tok(d),
        out_shape=jax.ShapeDtypeStruct((nb, nq * CH, d), F32),
        compiler_params=_cparams("parallel", "parallel"),
        name="merge_and_mlp",
    )(z, mod, g_mix, g_ff, oa, ob, oc, wg, wbr, wo, w1, w2)


def _rope_tables(seq, ctx_len):
    pos = jnp.arange(seq, dtype=jnp.int32)
    row = (pos // GRID_W).astype(F32)
    col = (pos % GRID_W).astype(F32)
    half = HEAD_DIM // 2
    inv_freq = jnp.power(ROPE_THETA, -jnp.arange(0, half, 2, dtype=F32) / half)
    ang_r = row[:, None] * inv_freq[None, :]
    ang_c = col[:, None] * inv_freq[None, :]
    cos = jnp.concatenate([jnp.cos(ang_r)] * 2 + [jnp.cos(ang_c)] * 2, axis=-1)
    sin = jnp.concatenate([-jnp.sin(ang_r), jnp.sin(ang_r), -jnp.sin(ang_c), jnp.sin(ang_c)], axis=-1)
    cos = jnp.concatenate([cos, jnp.ones((ctx_len, HEAD_DIM), F32)], axis=0)
    sin = jnp.concatenate([sin, jnp.zeros((ctx_len, HEAD_DIM), F32)], axis=0)
    nt = (seq + ctx_len) // CH
    to_t = lambda a: a.reshape(nt, CH, HEAD_DIM).transpose(0, 2, 1)
    rope_t = jnp.stack([to_t(cos), to_t(sin)], axis=1)
    lower = (jnp.arange(HEAD_DIM) % (half)) < (half // 2)
    pair = lambda a: jnp.concatenate([a, a], axis=-1)
    rope_n = jnp.stack([pair(cos), pair(jnp.where(lower, sin, 0.0)),
                        pair(jnp.where(lower, 0.0, sin))], axis=0)
    return rope_t, rope_n


def kernel(x, c, ctx, c_ctx, w_ada, b_ada, g_pre_mix, g_post_mix, g_pre_ff, g_post_ff, w_in, g_qnorm,
           g_knorm, lam_q1, lam_k1, lam_q2, lam_k2, g_subln, sink, w_branch, w_out, w_ff1, w_ff2):
    nb, seq, d = x.shape
    ctx_len = ctx.shape[1]
    depth = w_ada.shape[0]
    assert seq % (KB * CH) == 0 and ctx_len == CH and nb + 1 <= MOD_ROWS and seq % GRID_W == 0
    nl = seq // CH
    nt = nl + 1

    z = jnp.concatenate([x, ctx], axis=1)
    cvec = jnp.zeros((MOD_ROWS, d), F32).at[:nb].set(c).at[nb].set(c_ctx)
    mods = _mod_call(cvec, w_ada, b_ada).reshape(depth, MOD_ROWS, 6, d)
    rope_t, rope_n = _rope_tables(seq, ctx_len)

    splits = (A_W, A_W, A_W, B_W, B_KW, B_KW, C_W, C_KW, C_KW)
    offs = [0]
    for s in splits:
        offs.append(offs[-1] + s)
    col = lambda w, i: w[:, offs[i]:offs[i + 1]]

    for l in range(depth):
        last = l == depth - 1
        nq = nl if last else nt
        lam_init = 0.8 - 0.6 * math.exp(-0.3 * l)
        w = w_in[l]
        wt = jnp.concatenate([col(w, i) for i in (0, 2, 3, 5, 6, 8)], axis=1).T.astype(BF16)
        wn = jnp.concatenate([col(w, i) for i in (1, 4, 7)], axis=1).astype(BF16)
        wg = w[:, offs[-1]:].astype(BF16)
        row = lambda g: g[l].reshape(1, -1)
        gq_t = jnp.broadcast_to(g_qnorm[l][:, None], (HEAD_DIM, CH))
        gk_n = jnp.concatenate([g_knorm[l]] * B_KV).reshape(1, KV_PAIR)
        gs_t = jnp.broadcast_to(g_subln[l][:, None], (2 * HEAD_DIM, CH))
        lam_p = jnp.stack([lam_q1[l], lam_k1[l], lam_q2[l], lam_k2[l]], axis=0)

        qa, va, qb, vb, qc, vc, ka, kb, kc = _inproj_call(
            z, mods[l], row(g_pre_mix), wt, wn, rope_t, rope_n, gq_t, gk_n, nl)
        oa = _attn_a_call(qa, ka, va, lam_p, gs_t, nl, nq, lam_init)
        ob = _attn_b_call(qb, kb, vb, nl, nq)
        oc = _attn_c_call(sink[l], qc, kc, vc, nl, nq)
        z = _merge_mlp_call(z, mods[l], jnp.stack([g_pre_mix[l], g_post_mix[l]]),
                            jnp.stack([g_pre_ff[l], g_post_ff[l]]), oa, ob, oc, wg,
                            w_branch[l].astype(BF16), w_out[l].astype(BF16),
                            w_ff1[l].astype(BF16), w_ff2[l].astype(BF16), nl, nq)
    return z
```

```python
import functools
import math

import jax
import jax.numpy as jnp
from jax import lax
from jax.experimental import pallas as pl
from jax.experimental.pallas import tpu as pltpu

HEAD_DIM = 64
A_HEADS = 4
B_HEADS = 8
B_KV = 2
C_HEADS = 8
C_KV = 2
WINDOW = 128
GRID_W = 64
ROPE_THETA = 10000.0
EPS = 1e-6
SUBLN_EPS = 1e-5

A_W = A_HEADS * 2 * HEAD_DIM
B_W = B_HEADS * HEAD_DIM
C_W = C_HEADS * HEAD_DIM
B_KW = B_KV * HEAD_DIM
C_KW = C_KV * HEAD_DIM
KV_PAIR = 2 * HEAD_DIM

CH = 256
KB_A = 2
KB_B = 1
NBUF = 4
A_STEP = 2
MOD_ROWS = 8
NEG = -1e30
LOG2E = math.log2(math.e)
DEN_ROWS = 16
VMEM_LIMIT = 48 * 1024 * 1024

F32 = jnp.float32
BF16 = jnp.bfloat16


def _cparams(*sem):
    return pltpu.CompilerParams(dimension_semantics=sem, vmem_limit_bytes=VMEM_LIMIT)


def _const_spec(shape):
    nd = len(shape)
    return pl.BlockSpec(shape, lambda *_: (0,) * nd, pipeline_mode=pl.Buffered(1))


def _rms(x, gain, eps):
    return x * lax.rsqrt(jnp.mean(x * x, axis=-1, keepdims=True) + eps) * gain


def _modulate(x, gain, shift, scale):
    return _rms(x, gain, EPS) * (1.0 + scale) + shift


def _mod_kernel(c_ref, w_ref, b_ref, o_ref):
    c = c_ref[...]
    s = c * jax.nn.sigmoid(c)
    o_ref[0] = jnp.dot(s, w_ref[0], preferred_element_type=F32,
                       precision=lax.Precision.HIGHEST) + b_ref[0]


def _mod_call(cvec, w_ada, b_ada):
    depth, d, n = w_ada.shape
    tn = n // 4
    return pl.pallas_call(
        _mod_kernel,
        grid=(depth, n // tn),
        in_specs=[pl.BlockSpec((MOD_ROWS, d), lambda l, j: (0, 0)),
                  pl.BlockSpec((1, d, tn), lambda l, j: (l, 0, j)),
                  pl.BlockSpec((1, 1, tn), lambda l, j: (l, 0, j))],
        out_specs=pl.BlockSpec((1, MOD_ROWS, tn), lambda l, j: (l, 0, j)),
        out_shape=jax.ShapeDtypeStruct((depth, MOD_ROWS, n), F32),
        compiler_params=_cparams("parallel", "parallel"),
        name="adaln_vectors",
    )(cvec, w_ada, b_ada.reshape(depth, 1, n))


def _rope_t(y3, cos_t, sin_t):
    part = jnp.concatenate([y3[:, 16:32], y3[:, 0:16], y3[:, 48:64], y3[:, 32:48]], axis=1)
    return y3 * cos_t[None] + part * sin_t[None]


def _rope_n(y, cos_n, sin_lo, sin_hi):
    return (y * cos_n + pltpu.roll(y, KV_PAIR - 16, axis=1) * sin_lo
            + pltpu.roll(y, 16, axis=1) * sin_hi)


def _inproj_kernel(z_ref, mod_ref, g_ref, wt_ref, wn_ref, rt_ref, rn_ref, gq_ref, gk_ref,
                   qa_ref, va_ref, qb_ref, vb_ref, qc_ref, vc_ref, ka_ref, kb_ref, kc_ref):
    m = mod_ref[...]
    xb = _modulate(z_ref[0], g_ref[...], m[0:1], m[1:2]).astype(BF16)
    cos_t, sin_t = rt_ref[0, 0], rt_ref[0, 1]
    cos_n, sin_lo, sin_hi = rn_ref[0], rn_ref[1], rn_ref[2]
    ch = xb.shape[0]
    q_scale = HEAD_DIM ** -0.5 * LOG2E

    all_t = lax.dot_general(wt_ref[...], xb, (((1,), (1,)), ((), ())), preferred_element_type=F32)

    def proj_t(lo, hi):
        return all_t[lo:hi]

    def store_q(ref, y3):
        ref[0, 0] = (_rope_t(y3, cos_t, sin_t) * q_scale).reshape(-1, ch).astype(BF16)

    o = 0
    store_q(qa_ref, proj_t(o, o + A_W).reshape(-1, HEAD_DIM, ch))
    o += A_W
    va_ref[0, 0] = proj_t(o, o + A_W).astype(BF16)
    o += A_W
    qb = proj_t(o, o + B_W).reshape(-1, HEAD_DIM, ch)
    qb = qb * lax.rsqrt(jnp.mean(qb * qb, axis=1, keepdims=True) + EPS) * gq_ref[...][None]
    store_q(qb_ref, qb)
    o += B_W
    vb_ref[0, 0] = proj_t(o, o + B_KW).astype(BF16)
    o += B_KW
    store_q(qc_ref, proj_t(o, o + C_W).reshape(-1, HEAD_DIM, ch))
    o += C_W
    vc_ref[0, 0] = proj_t(o, o + C_KW).astype(BF16)

    kn = jnp.dot(xb, wn_ref[...], preferred_element_type=F32)
    for i in range(A_W // KV_PAIR):
        ka_ref[0, :, i * KV_PAIR:(i + 1) * KV_PAIR] = _rope_n(
            kn[:, i * KV_PAIR:(i + 1) * KV_PAIR], cos_n, sin_lo, sin_hi).astype(BF16)
    kb = kn[:, A_W:A_W + B_KW]
    sq = kb * kb
    first = lax.broadcasted_iota(jnp.int32, sq.shape, 1) < HEAD_DIM
    s0 = jnp.sum(jnp.where(first, sq, 0.0), axis=-1, keepdims=True)
    s1 = jnp.sum(jnp.where(first, 0.0, sq), axis=-1, keepdims=True)
    ms = jnp.where(first, s0, s1) * (1.0 / HEAD_DIM)
    kb = kb * lax.rsqrt(ms + EPS) * gk_ref[...]
    kb_ref[0] = _rope_n(kb, cos_n, sin_lo, sin_hi).astype(BF16)
    kc_ref[0] = _rope_n(kn[:, A_W + B_KW:], cos_n, sin_lo, sin_hi).astype(BF16)


def _mod_index(nl, nb):
    return lambda b, j: (jnp.where(j < nl, b, nb), 0, 0)


def _inproj_call(z, mod, g_pre, wt, wn, rope_t, rope_n, gq_t, gk_n, nl):
    nb, t, d = z.shape
    nt = t // CH
    tok = lambda w: pl.BlockSpec((1, 1, w, CH), lambda b, j: (b, j, 0, 0))
    nat = lambda w: pl.BlockSpec((1, CH, w), lambda b, j: (b, j, 0))
    tshape = lambda w: jax.ShapeDtypeStruct((nb, nt, w, CH), BF16)
    nshape = lambda w: jax.ShapeDtypeStruct((nb, t, w), BF16)
    return pl.pallas_call(
        _inproj_kernel,
        grid=(nb, nt),
        in_specs=[pl.BlockSpec((1, CH, d), lambda b, j: (b, j, 0)),
                  pl.BlockSpec((None, 6, d), _mod_index(nl, nb)),
                  _const_spec((1, d)),
                  _const_spec(wt.shape),
                  _const_spec(wn.shape),
                  pl.BlockSpec((1, 2, HEAD_DIM, CH), lambda b, j: (j, 0, 0, 0)),
                  pl.BlockSpec((3, CH, KV_PAIR), lambda b, j: (0, j, 0)),
                  _const_spec(gq_t.shape),
                  _const_spec(gk_n.shape)],
        out_specs=[tok(A_W), tok(A_W), tok(B_W), tok(B_KW), tok(C_W), tok(C_KW),
                   nat(A_W), nat(B_KW), nat(C_KW)],
        out_shape=[tshape(A_W), tshape(A_W), tshape(B_W), tshape(B_KW), tshape(C_W), tshape(C_KW),
                   nshape(A_W), nshape(B_KW), nshape(C_KW)],
        compiler_params=_cparams("parallel", "parallel"),
        name="in_projection",
    )(z, mod, g_pre, wt, wn, rope_t, rope_n, gq_t, gk_n)


def _with_ones(v_t):
    return jnp.concatenate([v_t, jnp.ones((DEN_ROWS, v_t.shape[1]), v_t.dtype)], axis=0)


def _softmax_pv(s, s_max, v_t, pv_fn, m_ref, acc_ref):
    m_prev = m_ref[...]
    m_new = jnp.maximum(m_prev, s_max)
    alpha = jnp.exp2(m_prev - m_new)
    p = jnp.exp2(s - m_new).astype(BF16)
    acc_ref[...] = alpha * acc_ref[...] + pv_fn(v_t, p)
    m_ref[...] = m_new


def _normalized(acc_ref):
    e = acc_ref.shape[0] - DEN_ROWS
    return acc_ref[:e, :] / acc_ref[e:e + 1, :]


def _cat_queries(q, heads, slot):
    parts = []
    for h in range(heads):
        qh = q[h * HEAD_DIM:(h + 1) * HEAD_DIM]
        z = jnp.zeros_like(qh)
        parts.append(jnp.concatenate([qh, z] if slot == 0 else [z, qh], axis=0))
    return jnp.concatenate(parts, axis=1)


def _heads_to_rows(o, heads):
    return jnp.concatenate([o[:, h * CH:(h + 1) * CH] for h in range(heads)], axis=0)


def _sweep_keys(is_latent, nl, kb, k_ref, v_ref, score_fn, pv_fn, m_ref, acc_ref, s_ref, smax_ref):
    m_ref[...] = jnp.full(m_ref.shape, NEG, F32)
    acc_ref[...] = jnp.zeros(acc_ref.shape, F32)
    nblk = nl // kb

    def put_scores(u, lo, size):
        s = score_fn(k_ref[0, lo:lo + size, :])
        s_ref[u, :size] = s
        smax_ref[u] = jnp.max(s, axis=0, keepdims=True)

    def fold(u, size, v_t):
        _softmax_pv(s_ref[u, :size], smax_ref[u], v_t, pv_fn, m_ref, acc_ref)

    @pl.when(is_latent)
    def _():
        put_scores(0, 0, kb * CH)
        for b in range(nblk):
            if b + 1 < nblk:
                put_scores((b + 1) % NBUF, (b + 1) * kb * CH, kb * CH)
            else:
                put_scores((b + 1) % NBUF, nl * CH, CH)
            fold(b % NBUF, kb * CH,
                 jnp.concatenate([v_ref[0, kb * b + c] for c in range(kb)], axis=1))
        fold(nblk % NBUF, CH, v_ref[0, nl])

    @pl.when(jnp.logical_not(is_latent))
    def _():
        put_scores(0, nl * CH, CH)
        fold(0, CH, v_ref[0, nl])


def _grouped_fns(q_cats, k_lanes, v_rows):
    widths = [qc.shape[1] for qc in q_cats]
    starts = [sum(widths[:g]) for g in range(len(widths))]

    def score_fn(k):
        return jnp.concatenate(
            [jnp.dot(k[:, k_lanes[g]], q_cats[g], preferred_element_type=F32)
             for g in range(len(q_cats))], axis=1)

    def pv_fn(v_t, p):
        return jnp.concatenate(
            [jnp.dot(_with_ones(v_t[v_rows[g]]), p[:, starts[g]:starts[g] + widths[g]],
                     preferred_element_type=F32) for g in range(len(q_cats))], axis=1)

    return score_fn, pv_fn


def _attn_a_kernel(nl, lam_init, q_ref, k_ref, v_ref, lam_ref, gs_ref, o_ref,
                   m_ref, acc_ref, s_ref, smax_ref):
    ew = 2 * HEAD_DIM
    q = q_ref[0, 0]
    q_cats = [jnp.concatenate([_cat_queries(q[h * ew:h * ew + HEAD_DIM], 1, 0),
                               _cat_queries(q[h * ew + HEAD_DIM:(h + 1) * ew], 1, 1)], axis=1)
              for h in range(A_STEP)]
    lanes = [slice(h * ew, (h + 1) * ew) for h in range(A_STEP)]
    score_fn, pv_fn = _grouped_fns(q_cats, lanes, lanes)
    _sweep_keys(pl.program_id(2) < nl, nl, KB_A, k_ref, v_ref, score_fn, pv_fn, m_ref, acc_ref,
                s_ref, smax_ref)

    lp = lam_ref[...]
    lam = (jnp.exp(jnp.sum(lp[0:1] * lp[1:2], axis=-1, keepdims=True))
           - jnp.exp(jnp.sum(lp[2:3] * lp[3:4], axis=-1, keepdims=True)) + lam_init)
    o = _normalized(acc_ref)
    outs = []
    for h in range(A_STEP):
        oh = o[:, 2 * h * CH:(2 * h + 1) * CH] - lam * o[:, (2 * h + 1) * CH:(2 * h + 2) * CH]
        oh = oh * lax.rsqrt(jnp.mean(oh * oh, axis=0, keepdims=True) + SUBLN_EPS) * gs_ref[...]
        outs.append(oh * (1.0 - lam_init))
    o_ref[0] = jnp.concatenate(outs, axis=0).T.astype(BF16)


def _attn_a_call(q_t, k, v_t, lam_p, gs_t, nl, nq, lam_init):
    nb, nt = q_t.shape[:2]
    t = k.shape[1]
    sw = A_STEP * 2 * HEAD_DIM
    qw = A_STEP * 2 * CH
    return pl.pallas_call(
        functools.partial(_attn_a_kernel, nl, lam_init),
        grid=(nb, A_HEADS // A_STEP, nq),
        in_specs=[pl.BlockSpec((1, 1, sw, CH), lambda b, h, i: (b, i, h, 0)),
                  pl.BlockSpec((1, t, sw), lambda b, h, i: (b, 0, h)),
                  pl.BlockSpec((1, nt, sw, CH), lambda b, h, i: (b, 0, h, 0)),
                  pl.BlockSpec(lam_p.shape, lambda b, h, i: (0, 0)),
                  pl.BlockSpec(gs_t.shape, lambda b, h, i: (0, 0))],
        out_specs=pl.BlockSpec((1, CH, sw), lambda b, h, i: (b, i, h)),
        out_shape=jax.ShapeDtypeStruct((nb, nq * CH, A_W), BF16),
        scratch_shapes=[pltpu.VMEM((1, qw), F32), pltpu.VMEM((2 * HEAD_DIM + DEN_ROWS, qw), F32),
                        pltpu.VMEM((NBUF, KB_A * CH, qw), F32), pltpu.VMEM((NBUF, 1, qw), F32)],
        compiler_params=_cparams("parallel", "parallel", "arbitrary"),
        name="attn_differential",
    )(q_t, k, v_t, lam_p, gs_t)


def _attn_b_kernel(nl, q_ref, k_ref, v_ref, o_ref, m_ref, acc_ref, s_ref, smax_ref):
    n = pl.program_id(1)
    group = B_HEADS // B_KV
    q = q_ref[0, 0]
    q_cat = jnp.where(n == 0, _cat_queries(q, group, 0), _cat_queries(q, group, 1))

    def score_fn(k):
        return jnp.dot(k, q_cat, preferred_element_type=F32)

    def pv_fn(v_t, p):
        return jnp.dot(_with_ones(v_t), p, preferred_element_type=F32)

    _sweep_keys(pl.program_id(2) < nl, nl, KB_B, k_ref, v_ref, score_fn, pv_fn, m_ref, acc_ref,
                s_ref, smax_ref)
    o_ref[0] = _heads_to_rows(_normalized(acc_ref), group).T.astype(BF16)


def _attn_b_call(q_t, k, v_t, nl, nq):
    nb, nt = q_t.shape[:2]
    t = k.shape[1]
    group = B_HEADS // B_KV
    gw = group * HEAD_DIM
    qw = group * CH
    return pl.pallas_call(
        functools.partial(_attn_b_kernel, nl),
        grid=(nb, B_KV, nq),
        in_specs=[pl.BlockSpec((1, 1, gw, CH), lambda b, n, i: (b, i, n, 0)),
                  pl.BlockSpec((1, t, KV_PAIR), lambda b, n, i: (b, 0, 0)),
                  pl.BlockSpec((1, nt, HEAD_DIM, CH), lambda b, n, i: (b, 0, n, 0))],
        out_specs=pl.BlockSpec((1, CH, gw), lambda b, n, i: (b, i, n)),
        out_shape=jax.ShapeDtypeStruct((nb, nq * CH, B_W), BF16),
        scratch_shapes=[pltpu.VMEM((1, qw), F32), pltpu.VMEM((HEAD_DIM + DEN_ROWS, qw), F32),
                        pltpu.VMEM((NBUF, KB_B * CH, qw), F32), pltpu.VMEM((NBUF, 1, qw), F32)],
        compiler_params=_cparams("parallel", "parallel", "arbitrary"),
        name="attn_dense_gqa",
    )(q_t, k, v_t)


def _gqa_fns(q, heads, kv):
    group = heads // kv
    gw = group * HEAD_DIM
    q_cat = jnp.concatenate([_cat_queries(q[n * gw:(n + 1) * gw], group, n) for n in range(kv)],
                            axis=1)

    def score_fn(k):
        return jnp.dot(k, q_cat, preferred_element_type=F32)

    def pv_fn(v_t, p):
        return jnp.concatenate(
            [jnp.dot(_with_ones(v_t[n * HEAD_DIM:(n + 1) * HEAD_DIM]),
                     p[:, n * group * CH:(n + 1) * group * CH], preferred_element_type=F32)
             for n in range(kv)], axis=1)

    return score_fn, pv_fn


def _attn_c_kernel(nl, sink_ref, q_ref, bias_ref, k0_ref, k1_ref, k2_ref, kx_ref,
                   v0_ref, v1_ref, v2_ref, vx_ref, o_ref, m_ref, acc_ref):
    qi = pl.program_id(1)
    hw = CH - WINDOW
    score_fn, pv_fn = _gqa_fns(q_ref[0, 0], C_HEADS, C_KV)
    k = jnp.concatenate([k0_ref[0, hw:, :], k1_ref[0], k2_ref[0, :WINDOW, :], kx_ref[0]], axis=0)
    v_t = jnp.concatenate([v0_ref[0, 0, :, hw:], v1_ref[0, 0], v2_ref[0, 0, :, :WINDOW], vx_ref[0, 0]],
                          axis=1)
    is_latent = qi < nl
    bias = bias_ref[...]
    bias = jnp.concatenate([
        jnp.where(is_latent & (qi >= 1), bias[:WINDOW], NEG),
        jnp.where(is_latent, bias[WINDOW:WINDOW + CH], NEG),
        jnp.where(is_latent & (qi + 1 < nl), bias[WINDOW + CH:], NEG),
        jnp.zeros((CH, CH), F32)], axis=0)
    s = score_fn(k)
    s = jnp.concatenate([s[:, h * CH:(h + 1) * CH] + bias for h in range(C_HEADS)], axis=1)
    m_ref[...] = jnp.concatenate(
        [jnp.full((1, CH), sink_ref[h] * LOG2E, F32) for h in range(C_HEADS)], axis=1)
    acc_ref[:HEAD_DIM, :] = jnp.zeros((HEAD_DIM, C_HEADS * CH), F32)
    acc_ref[HEAD_DIM:, :] = jnp.ones((DEN_ROWS, C_HEADS * CH), F32)
    _softmax_pv(s, jnp.max(s, axis=0, keepdims=True), v_t, pv_fn, m_ref, acc_ref)
    o_ref[0] = _heads_to_rows(_normalized(acc_ref), C_HEADS).T.astype(BF16)


def _window_bias():
    key = jnp.arange(-WINDOW, CH + WINDOW, dtype=jnp.int32)[:, None]
    qry = jnp.arange(CH, dtype=jnp.int32)[None, :]
    return jnp.where(jnp.abs(key - qry) <= WINDOW, 0.0, NEG).astype(F32)


def _attn_c_call(sink, q_t, k, v_t, nl, nq):
    nb, nt = q_t.shape[:2]
    bias = _window_bias()
    band = lambda c: (lambda b, i: (b, jnp.clip(i + (c - 1), 0, nl - 1), 0))
    band_t = lambda c: (lambda b, i: (b, jnp.clip(i + (c - 1), 0, nl - 1), 0, 0))
    k_spec = lambda f: pl.BlockSpec((1, CH, KV_PAIR), f)
    v_spec = lambda f: pl.BlockSpec((1, 1, C_KW, CH), f)
    qw = C_HEADS * CH
    return pl.pallas_call(
        functools.partial(_attn_c_kernel, nl),
        grid=(nb, nq),
        in_specs=[pl.BlockSpec(memory_space=pltpu.SMEM),
                  pl.BlockSpec((1, 1, C_W, CH), lambda b, i: (b, i, 0, 0)),
                  _const_spec(bias.shape),
                  k_spec(band(0)), k_spec(band(1)), k_spec(band(2)),
                  k_spec(lambda b, i: (b, nl, 0)),
                  v_spec(band_t(0)), v_spec(band_t(1)), v_spec(band_t(2)),
                  v_spec(lambda b, i: (b, nl, 0, 0))],
        out_specs=pl.BlockSpec((1, CH, C_W), lambda b, i: (b, i, 0)),
        out_shape=jax.ShapeDtypeStruct((nb, nq * CH, C_W), BF16),
        scratch_shapes=[pltpu.VMEM((1, qw), F32), pltpu.VMEM((HEAD_DIM + DEN_ROWS, qw), F32)],
        compiler_params=_cparams("parallel", "parallel"),
        name="attn_window",
    )(sink, q_t, bias, k, k, k, k, v_t, v_t, v_t, v_t)


def _merge_mlp_kernel(z_ref, mod_ref, gmix_ref, gff_ref, oa_ref, ob_ref, oc_ref,
                      wg_ref, wbr_ref, wo_ref, w1_ref, w2_ref, out_ref):
    x = z_ref[0]
    m = mod_ref[...]
    d = x.shape[-1]
    xb = _modulate(x, gmix_ref[0:1], m[0:1], m[1:2]).astype(BF16)
    mixed = None
    row = 0
    for i, o_ref in enumerate((oa_ref, ob_ref, oc_ref)):
        w = o_ref.shape[-1]
        gate = jax.nn.sigmoid(jnp.dot(xb, wg_ref[:, i * d:(i + 1) * d], preferred_element_type=F32))
        term = gate * jnp.dot(o_ref[0], wbr_ref[row:row + w, :], preferred_element_type=F32)
        mixed = term if mixed is None else mixed + term
        row += w
    mix = jnp.dot(mixed.astype(BF16), wo_ref[...], preferred_element_type=F32)
    x = x + m[2:3] * _rms(mix, gmix_ref[1:2], EPS)

    xb = _modulate(x, gff_ref[0:1], m[3:4], m[4:5]).astype(BF16)
    h = jnp.maximum(jnp.dot(xb, w1_ref[...], preferred_element_type=F32), 0.0)
    y = jnp.dot((h * h).astype(BF16), w2_ref[...], preferred_element_type=F32)
    out_ref[0] = x + m[5:6] * _rms(y, gff_ref[1:2], EPS)


def _merge_mlp_call(z, mod, g_mix, g_ff, oa, ob, oc, wg, wbr, wo, w1, w2, nl, nq):
    nb, _, d = z.shape
    tok = lambda w: pl.BlockSpec((1, CH, w), lambda b, j: (b, j, 0))
    return pl.pallas_call(
        _merge_mlp_kernel,
        grid=(nb, nq),
        in_specs=[tok(d), pl.BlockSpec((None, 6, d), _mod_index(nl, nb)),
                  _const_spec(g_mix.shape), _const_spec(g_ff.shape),
                  tok(A_W), tok(B_W), tok(C_W),
                  _const_spec(wg.shape), _const_spec(wbr.shape), _const_spec(wo.shape),
                  _const_spec(w1.shape), _const_spec(w2.shape)],
        out_specs=tok(d),
        out_shape=jax.ShapeDtypeStruct((nb, nq * CH, d), F32),
        compiler_params=_cparams("parallel", "parallel"),
        name="merge_and_mlp",
    )(z, mod, g_mix, g_ff, oa, ob, oc, wg, wbr, wo, w1, w2)


def _rope_tables(seq, ctx_len):
    pos = jnp.arange(seq, dtype=jnp.int32)
    row = (pos // GRID_W).astype(F32)
    col = (pos % GRID_W).astype(F32)
    half = HEAD_DIM // 2
    inv_freq = jnp.power(ROPE_THETA, -jnp.arange(0, half, 2, dtype=F32) / half)
    ang_r = row[:, None] * inv_freq[None, :]
    ang_c = col[:, None] * inv_freq[None, :]
    cos = jnp.concatenate([jnp.cos(ang_r)] * 2 + [jnp.cos(ang_c)] * 2, axis=-1)
    sin = jnp.concatenate([-jnp.sin(ang_r), jnp.sin(ang_r), -jnp.sin(ang_c), jnp.sin(ang_c)], axis=-1)
    cos = jnp.concatenate([cos, jnp.ones((ctx_len, HEAD_DIM), F32)], axis=0)
    sin = jnp.concatenate([sin, jnp.zeros((ctx_len, HEAD_DIM), F32)], axis=0)
    nt = (seq + ctx_len) // CH
    to_t = lambda a: a.reshape(nt, CH, HEAD_DIM).transpose(0, 2, 1)
    rope_t = jnp.stack([to_t(cos), to_t(sin)], axis=1)
    lower = (jnp.arange(HEAD_DIM) % (half)) < (half // 2)
    pair = lambda a: jnp.concatenate([a, a], axis=-1)
    rope_n = jnp.stack([pair(cos), pair(jnp.where(lower, sin, 0.0)),
                        pair(jnp.where(lower, 0.0, sin))], axis=0)
    return rope_t, rope_n


def kernel(x, c, ctx, c_ctx, w_ada, b_ada, g_pre_mix, g_post_mix, g_pre_ff, g_post_ff, w_in, g_qnorm,
           g_knorm, lam_q1, lam_k1, lam_q2, lam_k2, g_subln, sink, w_branch, w_out, w_ff1, w_ff2):
    nb, seq, d = x.shape
    ctx_len = ctx.shape[1]
    depth = w_ada.shape[0]
    assert seq % (max(KB_A, KB_B) * CH) == 0 and ctx_len == CH and nb + 1 <= MOD_ROWS and seq % GRID_W == 0
    nl = seq // CH
    nt = nl + 1

    z = jnp.concatenate([x, ctx], axis=1)
    cvec = jnp.zeros((MOD_ROWS, d), F32).at[:nb].set(c).at[nb].set(c_ctx)
    mods = _mod_call(cvec, w_ada, b_ada).reshape(depth, MOD_ROWS, 6, d)
    rope_t, rope_n = _rope_tables(seq, ctx_len)

    splits = (A_W, A_W, A_W, B_W, B_KW, B_KW, C_W, C_KW, C_KW)
    offs = [0]
    for s in splits:
        offs.append(offs[-1] + s)
    col = lambda w, i: w[:, offs[i]:offs[i + 1]]

    for l in range(depth):
        last = l == depth - 1
        nq = nl if last else nt
        lam_init = 0.8 - 0.6 * math.exp(-0.3 * l)
        w = w_in[l]
        wt = jnp.concatenate([col(w, i) for i in (0, 2, 3, 5, 6, 8)], axis=1).T.astype(BF16)
        wn = jnp.concatenate([col(w, i) for i in (1, 4, 7)], axis=1).astype(BF16)
        wg = w[:, offs[-1]:].astype(BF16)
        row = lambda g: g[l].reshape(1, -1)
        gq_t = jnp.broadcast_to(g_qnorm[l][:, None], (HEAD_DIM, CH))
        gk_n = jnp.concatenate([g_knorm[l]] * B_KV).reshape(1, KV_PAIR)
        gs_t = jnp.broadcast_to(g_subln[l][:, None], (2 * HEAD_DIM, CH))
        lam_p = jnp.stack([lam_q1[l], lam_k1[l], lam_q2[l], lam_k2[l]], axis=0)

        qa, va, qb, vb, qc, vc, ka, kb, kc = _inproj_call(
            z, mods[l], row(g_pre_mix), wt, wn, rope_t, rope_n, gq_t, gk_n, nl)
        oa = _attn_a_call(qa, ka, va, lam_p, gs_t, nl, nq, lam_init)
        ob = _attn_b_call(qb, kb, vb, nl, nq)
        oc = _attn_c_call(sink[l], qc, kc, vc, nl, nq)
        z = _merge_mlp_call(z, mods[l], jnp.stack([g_pre_mix[l], g_post_mix[l]]),
                            jnp.stack([g_pre_ff[l], g_post_ff[l]]), oa, ob, oc, wg,
                            w_branch[l].astype(BF16), w_out[l].astype(BF16),
                            w_ff1[l].astype(BF16), w_ff2[l].astype(BF16), nl, nq)
    return z
```
